```python
import math
import jax, jax.numpy as jnp
from jax import lax
import numpy as np

D_MODEL = 1024
BATCH = 8
SEQ = 2048
DEPTH = 1
DEC_BATCH = 128
DEC_SEQ = 4
PAST_LEN = 16384
PAGE_SIZE = 128

D_MIX = D_MODEL
D_A = D_MIX // 2
D_B = D_MIX - D_A
H_A = 4
DH_A = D_A // H_A
H_B = 4
DK_B = D_B // H_B
DV_B = D_B // H_B
CONV_W = 4
CHUNK = 64
D_FF = -(-8 * D_MODEL // (3 * 256)) * 256
EPS = 1e-6
SPLITS = [2 * D_A, D_A, D_A, H_A, H_A, D_B, D_B, D_B, D_B]
P_IN = sum(SPLITS)
F_A_OFF = 2 * D_A + D_A + D_A + H_A

kernel_name = "hymba_mlstm_hgrn2_decode_step"


def _rmsnorm(x, g):
    xf = x.astype(jnp.float32)
    y = xf * lax.rsqrt(jnp.mean(xf * xf, axis=-1, keepdims=True) + EPS)
    return (y * g.astype(jnp.float32)).astype(x.dtype)


def _headnorm(h, g, n_heads):
    B, T, W = h.shape
    hh = h.reshape(B, T, n_heads, W // n_heads)
    hh = hh * lax.rsqrt(jnp.mean(hh * hh, axis=-1, keepdims=True) + EPS)
    return hh.reshape(B, T, W) * g.astype(jnp.float32)


def _heads(x, n_heads):
    B, T, W = x.shape
    return x.reshape(B, T, n_heads, W // n_heads).transpose(0, 2, 1, 3).astype(jnp.float32)


def _to_chunks(x, L):
    B, H, T = x.shape[:3]
    return jnp.moveaxis(x.reshape(B, H, T // L, L, *x.shape[3:]), 2, 0)


def _from_chunks(y):
    nc, B, H, L = y.shape[:4]
    return jnp.moveaxis(y, 0, 2).reshape(B, H, nc * L, *y.shape[4:])


def _mlstm_chunk(carry, inp):
    C, n, m = carry
    q, k, v, ig, lf = inp
    L = q.shape[2]
    mask = jnp.tril(jnp.ones((L, L), dtype=bool))
    F = jnp.cumsum(lf, axis=-1)
    a = F + m[..., None]
    D = jnp.where(mask, F[..., :, None] - F[..., None, :] + ig[..., None, :], -jnp.inf)
    m_t = jnp.maximum(a, jnp.max(D, axis=-1))
    w_inter = jnp.exp(a - m_t)
    W = jnp.exp(D - m_t[..., None])
    s_qk = jnp.einsum('bhtk,bhsk->bhts', q, k) * W
    num = w_inter[..., None] * jnp.einsum('bhtk,bhkv->bhtv', q, C) + jnp.einsum('bhts,bhsv->bhtv', s_qk, v)
    den = w_inter * jnp.einsum('bhtk,bhk->bht', q, n) + jnp.sum(s_qk, axis=-1)
    h = num / jnp.maximum(jnp.abs(den), jnp.exp(-m_t))[..., None]
    decay = w_inter[..., -1]
    w_last = W[..., -1, :]
    C_new = decay[..., None, None] * C + jnp.einsum('bhs,bhsk,bhsv->bhkv', w_last, k, v)
    n_new = decay[..., None] * n + jnp.einsum('bhs,bhsk->bhk', w_last, k)
    return (C_new, n_new, m_t[..., -1]), h


def _hgrn_chunk(S, inp):
    q, k, lf, v = inp
    L = q.shape[2]
    mask = jnp.tril(jnp.ones((L, L), dtype=bool))
    b = jnp.cumsum(lf, axis=2)
    inter = jnp.einsum('bhtk,bhkv->bhtv', q * jnp.exp(b), S)
    dec = jnp.exp(jnp.where(mask[:, :, None], b[:, :, :, None, :] - b[:, :, None, :, :], -jnp.inf))
    A = jnp.einsum('bhtk,bhsk,bhtsk->bhts', q, k, dec)
    o = inter + jnp.einsum('bhts,bhsv->bhtv', A, v)
    bL = b[:, :, -1:, :]
    S_new = jnp.exp(bL[:, :, 0, :])[..., None] * S + jnp.einsum('bhsk,bhsv->bhkv', k * jnp.exp(bL - b), v)
    return S_new, o


def _mlstm_mixer(qk_raw, v_raw, o_raw, i_raw, f_raw, conv_buf, C0, n0, m0, conv_w, conv_b, norm_g):
    B, T, _ = qk_raw.shape
    xp = jnp.concatenate([conv_buf.astype(qk_raw.dtype), qk_raw], axis=1)
    conv = conv_b
    for j in range(CONV_W):
        conv = conv + conv_w[j] * xp[:, j:j + T]
    new_buf = xp[:, T:]
    qk = jax.nn.silu(conv)
    q = _heads(qk[..., :D_A], H_A) * (DH_A ** -0.5)
    k = _heads(qk[..., D_A:], H_A)
    v = _heads(v_raw, H_A)
    ig = i_raw.astype(jnp.float32).transpose(0, 2, 1)
    lf = jax.nn.log_sigmoid(f_raw.astype(jnp.float32)).transpose(0, 2, 1)
    L = math.gcd(T, CHUNK)
    carry0 = (C0.astype(jnp.float32), n0.astype(jnp.float32), m0.astype(jnp.float32))
    (C, n, m), h = lax.scan(_mlstm_chunk, carry0,
                            (_to_chunks(q, L), _to_chunks(k, L), _to_chunks(v, L), _to_chunks(ig, L), _to_chunks(lf, L)))
    h = _from_chunks(h).transpose(0, 2, 1, 3).reshape(B, T, D_A)
    out = jax.nn.sigmoid(o_raw.astype(jnp.float32)) * _headnorm(h, norm_g, H_A)
    return out, new_buf, C, n, m


def _hgrn_mixer(q_raw, f_raw, i_raw, g_raw, S0, lb, norm_g):
    B, T, _ = q_raw.shape
    fr = f_raw.astype(jnp.float32)
    f = lb + (1.0 - lb) * jax.nn.sigmoid(fr)
    lf = jnp.log(f)
    kk = (1.0 - lb) * jax.nn.sigmoid(-fr)
    q = _heads(jax.nn.silu(q_raw), H_B)
    k = _heads(kk, H_B)
    lf = _heads(lf, H_B)
    v = _heads(i_raw, H_B)
    L = math.gcd(T, CHUNK)
    S, o = lax.scan(_hgrn_chunk, S0.astype(jnp.float32),
                    (_to_chunks(q, L), _to_chunks(k, L), _to_chunks(lf, L), _to_chunks(v, L)))
    o = _from_chunks(o).transpose(0, 2, 1, 3).reshape(B, T, D_B)
    out = _headnorm(o, norm_g, H_B) * jax.nn.silu(g_raw.astype(jnp.float32))
    return out, S


def _layer(x, conv_buf, C0, n0, m0, S0, lb, norm_mix, w_in, b_in, conv_w, conv_b, mlstm_norm,
           hgrn_norm, w_out, norm_ffn, w_gate, w_up, w_down):
    h = _rmsnorm(x, norm_mix)
    p = jnp.einsum('btd,dp->btp', h, w_in) + b_in
    offs = [int(o) for o in np.cumsum(SPLITS)[:-1]]
    qk_a, v_a, o_a, i_a, f_a, q_b, f_b, i_b, g_b = jnp.split(p, offs, axis=-1)
    out_a, buf, C, n, m = _mlstm_mixer(qk_a, v_a, o_a, i_a, f_a, conv_buf, C0, n0, m0, conv_w, conv_b, mlstm_norm)
    out_b, S = _hgrn_mixer(q_b, f_b, i_b, g_b, S0, lb, hgrn_norm)
    mix = jnp.concatenate([out_a, out_b], axis=-1).astype(x.dtype)
    x = x + jnp.einsum('btm,md->btd', mix, w_out)
    h2 = _rmsnorm(x, norm_ffn)
    ff = jax.nn.silu(jnp.einsum('btd,df->btf', h2, w_gate)) * jnp.einsum('btd,df->btf', h2, w_up)
    x = x + jnp.einsum('btf,fd->btd', ff, w_down)
    return x, buf, C, n, m, S


def setup_inputs(seed: int = 0) -> dict:
    key = jax.random.key(seed)
    ks = jax.random.split(key, 24)
    f32 = jnp.float32
    nrm = lambda k, shape, s: jax.random.normal(k, shape, f32) * s
    b_in = nrm(ks[3], (DEPTH, P_IN), 0.02)
    b_in = b_in.at[:, F_A_OFF:F_A_OFF + H_A].add(3.0)
    return {
        "x_prompt": nrm(ks[0], (BATCH, SEQ, D_MODEL), 1.0),
        "x_sample": nrm(ks[1], (DEC_BATCH, DEC_SEQ, D_MODEL), 1.0),
        "state_conv": nrm(ks[4], (DEPTH, DEC_BATCH, CONV_W - 1, 2 * D_A), 1.0),
        "state_mlstm_C": nrm(ks[5], (DEPTH, DEC_BATCH, H_A, DH_A, DH_A), 0.3),
        "state_mlstm_n": nrm(ks[6], (DEPTH, DEC_BATCH, H_A, DH_A), 0.3),
        "state_mlstm_m": nrm(ks[7], (DEPTH, DEC_BATCH, H_A), 1.0),
        "state_hgrn_S": nrm(ks[8], (DEPTH, DEC_BATCH, H_B, DK_B, DV_B), 0.3),
        "norm_mix": 1.0 + nrm(ks[9], (DEPTH, D_MODEL), 0.02),
        "w_in": nrm(ks[2], (DEPTH, D_MODEL, P_IN), D_MODEL ** -0.5),
        "b_in": b_in,
        "conv_w": nrm(ks[10], (DEPTH, CONV_W, 2 * D_A), CONV_W ** -0.5),
        "conv_b": nrm(ks[11], (DEPTH, 2 * D_A), 0.02),
        "mlstm_norm": 1.0 + nrm(ks[12], (DEPTH, D_A), 0.02),
        "hgrn_lb_logits": nrm(ks[13], (DEPTH + 1, D_B), 0.5),
        "hgrn_norm": 1.0 + nrm(ks[14], (DEPTH, D_B), 0.02),
        "w_out": nrm(ks[15], (DEPTH, D_MIX, D_MODEL), D_MIX ** -0.5),
        "norm_ffn": 1.0 + nrm(ks[16], (DEPTH, D_MODEL), 0.02),
        "w_gate": nrm(ks[17], (DEPTH, D_MODEL, D_FF), D_MODEL ** -0.5),
        "w_up": nrm(ks[18], (DEPTH, D_MODEL, D_FF), D_MODEL ** -0.5),
        "w_down": nrm(ks[19], (DEPTH, D_FF, D_MODEL), D_FF ** -0.5),
        "norm_final": 1.0 + nrm(ks[20], (D_MODEL,), 0.02),
    }


def reference(x_prompt, x_sample, state_conv, state_mlstm_C, state_mlstm_n, state_mlstm_m, state_hgrn_S,
              norm_mix, w_in, b_in, conv_w, conv_b, mlstm_norm, hgrn_lb_logits, hgrn_norm, w_out,
              norm_ffn, w_gate, w_up, w_down, norm_final):
    lb_all = jnp.cumsum(jax.nn.softmax(hgrn_lb_logits.astype(jnp.float32), axis=0), axis=0)
    B = x_prompt.shape[0]
    xp, xs = x_prompt, x_sample
    p_states = [[], [], [], [], []]
    s_states = [[], [], [], [], []]
    for l in range(DEPTH):
        lw = (lb_all[l], norm_mix[l], w_in[l], b_in[l], conv_w[l], conv_b[l], mlstm_norm[l],
              hgrn_norm[l], w_out[l], norm_ffn[l], w_gate[l], w_up[l], w_down[l])
        zeros_buf = jnp.zeros((B, CONV_W - 1, 2 * D_A), xp.dtype)
        zeros_C = jnp.zeros((B, H_A, DH_A, DH_A), jnp.float32)
        zeros_n = jnp.zeros((B, H_A, DH_A), jnp.float32)
        zeros_m = jnp.zeros((B, H_A), jnp.float32)
        zeros_S = jnp.zeros((B, H_B, DK_B, DV_B), jnp.float32)
        xp, *sp = _layer(xp, zeros_buf, zeros_C, zeros_n, zeros_m, zeros_S, *lw)
        xs, *ss = _layer(xs, state_conv[l], state_mlstm_C[l], state_mlstm_n[l], state_mlstm_m[l],
                         state_hgrn_S[l], *lw)
        for j in range(5):
            p_states[j].append(sp[j])
            s_states[j].append(ss[j])
    y_prompt = _rmsnorm(xp, norm_final)
    y_sample = _rmsnorm(xs, norm_final)
    conv_p, C_p, n_p, m_p, S_p = [jnp.stack(s, axis=0) for s in p_states]
    conv_s, C_s, n_s, m_s, S_s = [jnp.stack(s, axis=0) for s in s_states]
    return (y_prompt, y_sample, conv_p, C_p, n_p, m_p, S_p, conv_s, C_s, n_s, m_s, S_s)
```

```python
import functools

import jax
import jax.numpy as jnp
from jax import lax
from jax.experimental import pallas as pl
from jax.experimental.pallas import tpu as pltpu

D_MODEL = 1024
D_A = 512
D_B = 512
N_HEADS = 4
DH = 128
CONV_W = 4
D_FF = 2816
EPS = 1e-6
P_MAIN = 2 * D_A + D_A + D_A + 4 * D_B
GATE_COLS = 128
SUBLANES = 8
CONV_PAD = 8
VMEM_LIMIT = 56 * 1024 * 1024

TT_PROMPT = 256
TT_SAMPLE = 8
TM_FFN = 512

_F32 = jnp.float32
_BF16 = jnp.bfloat16


def _dot(a, b):
    return jnp.dot(a, b, preferred_element_type=_F32)


def _dot_nt(a, b):
    return lax.dot_general(a, b, (((1,), (1,)), ((), ())), preferred_element_type=_F32)


def _dot_tn(a, b):
    return lax.dot_general(a, b, (((0,), (0,)), ((), ())), preferred_element_type=_F32)


def _sigmoid(x):
    return 1.0 / (1.0 + jnp.exp(-x))


def _rmsnorm(x, g):
    return x * lax.rsqrt(jnp.mean(x * x, axis=-1, keepdims=True) + EPS) * g


def _tril_matmul_cumsum(x, tril_bf16):
    hi = x.astype(_BF16)
    r1 = x - hi.astype(_F32)
    mid = r1.astype(_BF16)
    lo = (r1 - mid.astype(_F32)).astype(_BF16)
    return _dot(tril_bf16, hi) + _dot(tril_bf16, mid) + _dot(tril_bf16, lo)


def _sublane_cumsum(x, row):
    for sh in (1, 2, 4):
        x = x + jnp.where(row >= sh, pltpu.roll(x, sh, 0), 0.0)
    return x


def _row_at(x, row, t):
    return jnp.sum(jnp.where(row == t, x, 0.0), axis=0, keepdims=True)


def _mlstm_head(q, k, v, igc, lfc, C0, n0, m0, tril, eye, row, t_last):
    lfr = jnp.sum(jnp.where(eye, lfc, 0.0), axis=0, keepdims=True)
    Fc = jnp.sum(jnp.where(tril, lfr, 0.0), axis=1, keepdims=True)
    uc = igc - Fc
    ur = jnp.sum(jnp.where(eye, uc, 0.0), axis=0, keepdims=True)
    a = Fc + m0
    D = jnp.where(tril, Fc + ur, -jnp.inf)
    mt = jnp.maximum(a, jnp.max(D, axis=1, keepdims=True))
    w_inter = jnp.exp(a - mt)
    W = jnp.exp(D - mt)
    qb = q.astype(_BF16)
    kb = k.astype(_BF16)
    vb = v.astype(_BF16)
    s = _dot_nt(qb, kb) * W
    num = w_inter * _dot(qb, C0.astype(_BF16)) + _dot(s.astype(_BF16), vb)
    den = w_inter * jnp.sum(q * n0, axis=1, keepdims=True) + jnp.sum(s, axis=1, keepdims=True)
    h = num / jnp.maximum(jnp.abs(den), jnp.exp(-mt))
    F_last = _row_at(Fc, row, t_last)
    m_last = _row_at(mt, row, t_last)
    decay = _row_at(w_inter, row, t_last)
    valid = row <= t_last
    wl = jnp.where(valid, jnp.exp(jnp.where(valid, F_last + uc - m_last, 0.0)), 0.0)
    kw = wl * k
    C_new = decay * C0 + _dot_tn(kw.astype(_BF16), vb)
    n_new = decay * n0 + jnp.sum(kw, axis=0, keepdims=True)
    return h, C_new, n_new, m_last


def _hgrn_head(q_raw, f_raw, i_raw, lb, S0, tril_bf16, eye_dh, xor_rc, row, t_last, b_ref, qt_ref, kt_ref):
    tt = q_raw.shape[0]
    f = lb + (1.0 - lb) * _sigmoid(f_raw)
    lf = jnp.log(f)
    kk = (1.0 - lb) * _sigmoid(-f_raw)
    q = q_raw * _sigmoid(q_raw)
    v = i_raw
    vb = v.astype(_BF16)
    if tt == SUBLANES:
        b = _sublane_cumsum(lf, row)
    else:
        b = _tril_matmul_cumsum(lf, tril_bf16)
    o = _dot((q * jnp.exp(b)).astype(_BF16), S0.astype(_BF16))

    fz = jnp.where((row & (SUBLANES - 1)) == 0, 0.0, f)
    o = o + jnp.sum(q * kk, axis=1, keepdims=True) * v
    E = fz
    for d in range(1, SUBLANES):
        if d > 1:
            E = E * pltpu.roll(fz, d - 1, 0)
        A = jnp.sum(q * pltpu.roll(kk, d, 0) * E, axis=1, keepdims=True)
        o = o + A * pltpu.roll(v, d, 0)

    if tt > SUBLANES:
        b_ref[...] = b
        A_far = None
        lvl = 0
        h = SUBLANES
        while h < tt:
            cols = slice(lvl * DH, (lvl + 1) * DH)
            for j in range(tt // h):
                rows = slice(j * h, (j + 1) * h)
                if j % 2 == 1:
                    b_mid = b_ref[pl.ds(j * h - 1, 1), :]
                    qt_ref[rows, cols] = q[rows, :] * jnp.exp(b[rows, :] - b_mid)
                    kt_ref[rows, cols] = jnp.zeros((h, DH), _F32)
                else:
                    b_mid = b_ref[pl.ds((j + 1) * h - 1, 1), :]
                    kt_ref[rows, cols] = kk[rows, :] * jnp.exp(b_mid - b[rows, :])
                    qt_ref[rows, cols] = jnp.zeros((h, DH), _F32)
            A_h = _dot_nt(qt_ref[:, cols].astype(_BF16), kt_ref[:, cols].astype(_BF16))
            if 2 * h < tt:
                A_h = jnp.where(xor_rc < 2 * h, A_h, 0.0)
            A_far = A_h if A_far is None else A_far + A_h
            lvl += 1
            h *= 2
        o = o + _dot(A_far.astype(_BF16), vb)

    bL = _row_at(b, row, t_last)
    valid = row <= t_last
    ks = jnp.where(valid, kk * jnp.exp(jnp.where(valid, bL - b, 0.0)), 0.0)
    bL_col = jnp.sum(jnp.where(eye_dh, bL, 0.0), axis=1, keepdims=True)
    S_new = jnp.exp(bL_col) * S0 + _dot_tn(ks.astype(_BF16), vb)
    return o, S_new


def _headnorm(h, g):
    return h * lax.rsqrt(jnp.mean(h * h, axis=-1, keepdims=True) + EPS) * g


def _mixers(p_ref, g, convw_ref, convb_ref, mnorm_ref, lbl_ref, hnorm_ref,
            mix_ref, convout_ref, C_in, n_in, m_in, S_in, C_out, n_out, m_out, S_out,
            convbuf, b_ref, qt_ref, kt_ref, tt, t_last):
    row = lax.broadcasted_iota(jnp.int32, (tt, 1), 0)
    ri = lax.broadcasted_iota(jnp.int32, (tt, tt), 0)
    ci = lax.broadcasted_iota(jnp.int32, (tt, tt), 1)
    tril = ci <= ri
    eye = ci == ri
    tril_bf16 = jnp.where(tril, 1.0, 0.0).astype(_BF16)
    xor_rc = ri ^ ci
    rd = lax.broadcasted_iota(jnp.int32, (DH, DH), 0)
    cd = lax.broadcasted_iota(jnp.int32, (DH, DH), 1)
    eye_dh = rd == cd

    convbuf[CONV_PAD:CONV_PAD + tt, :] = p_ref[:, 0:2 * D_A]
    conv = convb_ref[...]
    for j in range(CONV_W):
        conv = conv + convw_ref[j:j + 1, :] * convbuf[pl.ds(CONV_PAD - (CONV_W - 1) + j, tt), :]
    new_rows = convbuf[pl.ds(CONV_PAD + t_last - (CONV_W - 2), CONV_W - 1), :]
    convout_ref[...] = new_rows
    convbuf[CONV_PAD - (CONV_W - 1):CONV_PAD, :] = new_rows
    qk = conv * _sigmoid(conv)

    lf_all = jnp.minimum(g, 0.0) - jnp.log1p(jnp.exp(-jnp.abs(g)))

    for hd in range(N_HEADS):
        cs = slice(hd * DH, (hd + 1) * DH)
        q = qk[:, hd * DH:(hd + 1) * DH] * (DH ** -0.5)
        k = qk[:, D_A + hd * DH:D_A + (hd + 1) * DH]
        v = p_ref[:, 2 * D_A + hd * DH:2 * D_A + (hd + 1) * DH]
        o_raw = p_ref[:, 3 * D_A + hd * DH:3 * D_A + (hd + 1) * DH]
        igc = g[:, hd:hd + 1]
        lfc = lf_all[:, N_HEADS + hd:N_HEADS + hd + 1]
        h, C_new, n_new, m_new = _mlstm_head(q, k, v, igc, lfc, C_in[hd], n_in[hd:hd + 1, :], m_in(hd),
                                             tril, eye, row, t_last)
        C_out[hd] = C_new
        n_out[hd:hd + 1, :] = n_new
        m_out[hd:hd + 1, :] = jnp.broadcast_to(m_new, (1, GATE_COLS))
        out_a = _sigmoid(o_raw) * _headnorm(h, mnorm_ref[:, cs])
        mix_ref[:, cs] = out_a.astype(mix_ref.dtype)

    logits = lbl_ref[...]
    ex = jnp.exp(logits - jnp.max(logits, axis=0, keepdims=True))
    lb_all = ex[0:1, :] / jnp.sum(ex, axis=0, keepdims=True)
    base = 4 * D_A
    for hd in range(N_HEADS):
        cs = slice(hd * DH, (hd + 1) * DH)
        q_raw = p_ref[:, base + hd * DH:base + (hd + 1) * DH]
        f_raw = p_ref[:, base + D_B + hd * DH:base + D_B + (hd + 1) * DH]
        i_raw = p_ref[:, base + 2 * D_B + hd * DH:base + 2 * D_B + (hd + 1) * DH]
        g_raw = p_ref[:, base + 3 * D_B + hd * DH:base + 3 * D_B + (hd + 1) * DH]
        o, S_new = _hgrn_head(q_raw, f_raw, i_raw, lb_all[:, cs], S_in[hd], tril_bf16, eye_dh, xor_rc, row,
                              t_last, b_ref, qt_ref, kt_ref)
        S_out[hd] = S_new
        out_b = _headnorm(o, hnorm_ref[:, cs]) * (g_raw * _sigmoid(g_raw))
        mix_ref[:, D_A + hd * DH:D_A + (hd + 1) * DH] = out_b.astype(mix_ref.dtype)


def _project(x, nm_ref, wmain_ref, bmain_ref, wg_ref, bg_ref):
    h = _rmsnorm(x, nm_ref[...]).astype(_BF16)
    p = _dot(h, wmain_ref[...]) + bmain_ref[...]
    g = _dot(h, wg_ref[...]) + bg_ref[...]
    return p, g


def _prompt_kernel(x_ref, nm_ref, wmain_ref, bmain_ref, wg_ref, bg_ref, convw_ref, convb_ref, mnorm_ref,
                   lbl_ref, hnorm_ref,
                   mix_ref, convp_ref, C_ref, n_ref, m_ref, S_ref,
                   p_scr, convbuf, b_ref, qt_ref, kt_ref):
    tt = x_ref.shape[0]

    @pl.when(pl.program_id(1) == 0)
    def _():
        C_ref[...] = jnp.zeros_like(C_ref)
        n_ref[...] = jnp.zeros_like(n_ref)
        m_ref[...] = jnp.zeros_like(m_ref)
        S_ref[...] = jnp.zeros_like(S_ref)
        convbuf[0:CONV_PAD, :] = jnp.zeros((CONV_PAD, 2 * D_A), _F32)

    p, g = _project(x_ref[...], nm_ref, wmain_ref, bmain_ref, wg_ref, bg_ref)
    p_scr[...] = p
    _mixers(p_scr, g, convw_ref, convb_ref, mnorm_ref, lbl_ref, hnorm_ref,
            mix_ref, convp_ref, C_ref, n_ref, lambda hd: m_ref[hd:hd + 1, 0:1], S_ref,
            C_ref, n_ref, m_ref, S_ref, convbuf, b_ref, qt_ref, kt_ref, tt, tt - 1)


def _proj_kernel(x_ref, nm_ref, wmain_ref, bmain_ref, wg_ref, bg_ref, p_ref, g_ref):
    p, g = _project(x_ref[...], nm_ref, wmain_ref, bmain_ref, wg_ref, bg_ref)
    p_ref[...] = p
    g_ref[...] = g


def _sample_kernel(p_ref, g_ref, conv0_ref, C0_ref, n0_ref, m0_ref, S0_ref,
                   convw_ref, convb_ref, mnorm_ref, lbl_ref, hnorm_ref,
                   mix_ref, convs_ref, C_ref, n_ref, m_ref, S_ref,
                   convbuf, t_last):
    tt = p_ref.shape[0]
    convbuf[0:CONV_PAD - (CONV_W - 1), :] = jnp.zeros((CONV_PAD - (CONV_W - 1), 2 * D_A), _F32)
    convbuf[CONV_PAD - (CONV_W - 1):CONV_PAD, :] = conv0_ref[...]
    m_ref[...] = jnp.zeros_like(m_ref)
    _mixers(p_ref, g_ref[...], convw_ref, convb_ref, mnorm_ref, lbl_ref, hnorm_ref,
            mix_ref, convs_ref, C0_ref, n0_ref, lambda hd: m0_ref[hd:hd + 1, :], S0_ref,
            C_ref, n_ref, m_ref, S_ref, convbuf, None, None, None, tt, t_last)


def _ffn_kernel(x_ref, mix_ref, wout_ref, nf_ref, wgate_ref, wup_ref, wdown_ref, nfin_ref, y_ref):
    x1 = x_ref[...] + _dot(mix_ref[...], wout_ref[...])
    h2 = _rmsnorm(x1, nf_ref[...]).astype(_BF16)
    gate = _dot(h2, wgate_ref[...])
    up = _dot(h2, wup_ref[...])
    ff = (gate * _sigmoid(gate) * up).astype(_BF16)
    x2 = x1 + _dot(ff, wdown_ref[...])
    y_ref[...] = _rmsnorm(x2, nfin_ref[...])


def _resident(shape):
    nd = len(shape)
    return pl.BlockSpec(shape, lambda *_: (0,) * nd, pipeline_mode=pl.Buffered(1))


def _prompt_mixer(x, nm, wmain, bmain, wg, bg, convw, convb, mnorm, lbl, hnorm):
    B, T, D = x.shape
    tt = TT_PROMPT
    nlev = (tt // SUBLANES).bit_length() - 1
    out_shape = (
        jax.ShapeDtypeStruct((B, T, D_MODEL), _BF16),
        jax.ShapeDtypeStruct((B, CONV_W - 1, 2 * D_A), _F32),
        jax.ShapeDtypeStruct((B, N_HEADS, DH, DH), _F32),
        jax.ShapeDtypeStruct((B, N_HEADS, DH), _F32),
        jax.ShapeDtypeStruct((B, SUBLANES, GATE_COLS), _F32),
        jax.ShapeDtypeStruct((B, N_HEADS, DH, DH), _F32),
    )
    in_specs = [
        pl.BlockSpec((None, tt, D), lambda b, t: (b, t, 0)),
        _resident(nm.shape), _resident(wmain.shape), _resident(bmain.shape), _resident(wg.shape),
        _resident(bg.shape), _resident(convw.shape), _resident(convb.shape), _resident(mnorm.shape),
        _resident(lbl.shape), _resident(hnorm.shape),
    ]
    out_specs = (
        pl.BlockSpec((None, tt, D_MODEL), lambda b, t: (b, t, 0)),
        pl.BlockSpec((None, CONV_W - 1, 2 * D_A), lambda b, t: (b, 0, 0)),
        pl.BlockSpec((None, N_HEADS, DH, DH), lambda b, t: (b, 0, 0, 0)),
        pl.BlockSpec((None, N_HEADS, DH), lambda b, t: (b, 0, 0)),
        pl.BlockSpec((None, SUBLANES, GATE_COLS), lambda b, t: (b, 0, 0)),
        pl.BlockSpec((None, N_HEADS, DH, DH), lambda b, t: (b, 0, 0, 0)),
    )
    scratch = [
        pltpu.VMEM((tt, P_MAIN), _F32),
        pltpu.VMEM((CONV_PAD + tt, 2 * D_A), _F32),
        pltpu.VMEM((tt, DH), _F32),
        pltpu.VMEM((tt, nlev * DH), _F32),
        pltpu.VMEM((tt, nlev * DH), _F32),
    ]
    return pl.pallas_call(
        _prompt_kernel, out_shape=out_shape, grid=(B, T // tt), in_specs=in_specs, out_specs=out_specs,
        scratch_shapes=scratch, name="prompt_mixer",
        compiler_params=pltpu.CompilerParams(dimension_semantics=("arbitrary", "arbitrary"),
                                             vmem_limit_bytes=VMEM_LIMIT),
    )(x, nm, wmain, bmain, wg, bg, convw, convb, mnorm, lbl, hnorm)


def _sample_proj(x, nm, wmain, bmain, wg, bg):
    N, D = x.shape
    tm = min(N, 512)
    return pl.pallas_call(
        _proj_kernel,
        out_shape=(jax.ShapeDtypeStruct((N, P_MAIN), _F32), jax.ShapeDtypeStruct((N, GATE_COLS), _F32)),
        grid=(N // tm,),
        in_specs=[pl.BlockSpec((tm, D), lambda i: (i, 0)), _resident(nm.shape), _resident(wmain.shape),
                  _resident(bmain.shape), _resident(wg.shape), _resident(bg.shape)],
        out_specs=(pl.BlockSpec((tm, P_MAIN), lambda i: (i, 0)), pl.BlockSpec((tm, GATE_COLS), lambda i: (i, 0))),
        name="sample_proj",
        compiler_params=pltpu.CompilerParams(dimension_semantics=("arbitrary",), vmem_limit_bytes=VMEM_LIMIT),
    )(x, nm, wmain, bmain, wg, bg)


def _sample_mixer(p, g, conv0, C0, n0, m0, S0, convw, convb, mnorm, lbl, hnorm, t_last):
    B, tt, _ = p.shape
    out_shape = (
        jax.ShapeDtypeStruct((B, tt, D_MODEL), _BF16),
        jax.ShapeDtypeStruct((B, CONV_W - 1, 2 * D_A), _F32),
        jax.ShapeDtypeStruct((B, N_HEADS, DH, DH), _F32),
        jax.ShapeDtypeStruct((B, N_HEADS, DH), _F32),
        jax.ShapeDtypeStruct((B, SUBLANES, GATE_COLS), _F32),
        jax.ShapeDtypeStruct((B, N_HEADS, DH, DH), _F32),
    )
    in_specs = [
        pl.BlockSpec((None, tt, P_MAIN), lambda b: (b, 0, 0)),
        pl.BlockSpec((None, tt, GATE_COLS), lambda b: (b, 0, 0)),
        pl.BlockSpec((None, CONV_W - 1, 2 * D_A), lambda b: (b, 0, 0)),
        pl.BlockSpec((None, N_HEADS, DH, DH), lambda b: (b, 0, 0, 0)),
        pl.BlockSpec((None, N_HEADS, DH), lambda b: (b, 0, 0)),
        pl.BlockSpec((None, N_HEADS, 1), lambda b: (b, 0, 0)),
        pl.BlockSpec((None, N_HEADS, DH, DH), lambda b: (b, 0, 0, 0)),
        _resident(convw.shape), _resident(convb.shape), _resident(mnorm.shape), _resident(lbl.shape),
        _resident(hnorm.shape),
    ]
    out_specs = (
        pl.BlockSpec((None, tt, D_MODEL), lambda b: (b, 0, 0)),
        pl.BlockSpec((None, CONV_W - 1, 2 * D_A), lambda b: (b, 0, 0)),
        pl.BlockSpec((None, N_HEADS, DH, DH), lambda b: (b, 0, 0, 0)),
        pl.BlockSpec((None, N_HEADS, DH), lambda b: (b, 0, 0)),
        pl.BlockSpec((None, SUBLANES, GATE_COLS), lambda b: (b, 0, 0)),
        pl.BlockSpec((None, N_HEADS, DH, DH), lambda b: (b, 0, 0, 0)),
    )
    return pl.pallas_call(
        functools.partial(_sample_kernel, t_last=t_last),
        out_shape=out_shape, grid=(B,), in_specs=in_specs, out_specs=out_specs,
        scratch_shapes=[pltpu.VMEM((CONV_PAD + tt, 2 * D_A), _F32)], name="sample_mixer",
        compiler_params=pltpu.CompilerParams(dimension_semantics=("arbitrary",), vmem_limit_bytes=VMEM_LIMIT),
    )(p, g, conv0, C0, n0, m0, S0, convw, convb, mnorm, lbl, hnorm)


def _ffn(x, mix, wout, nf, wgate, wup, wdown, nfin):
    N, D = x.shape
    tm = min(N, TM_FFN)
    return pl.pallas_call(
        _ffn_kernel, out_shape=jax.ShapeDtypeStruct((N, D), _F32), grid=(N // tm,),
        in_specs=[pl.BlockSpec((tm, D), lambda i: (i, 0)), pl.BlockSpec((tm, D_MODEL), lambda i: (i, 0)),
                  _resident(wout.shape), _resident(nf.shape), _resident(wgate.shape), _resident(wup.shape),
                  _resident(wdown.shape), _resident(nfin.shape)],
        out_specs=pl.BlockSpec((tm, D), lambda i: (i, 0)), name="ffn",
        compiler_params=pltpu.CompilerParams(dimension_semantics=("arbitrary",), vmem_limit_bytes=VMEM_LIMIT),
    )(x, mix, wout, nf, wgate, wup, wdown, nfin)


def kernel(x_prompt, x_sample, state_conv, state_mlstm_C, state_mlstm_n, state_mlstm_m, state_hgrn_S,
           norm_mix, w_in, b_in, conv_w, conv_b, mlstm_norm, hgrn_lb_logits, hgrn_norm, w_out,
           norm_ffn, w_gate, w_up, w_down, norm_final):
    depth = w_in.shape[0]
    assert depth == 1, "single-layer stack"
    B, T, D = x_prompt.shape
    BS, TS, _ = x_sample.shape
    l = 0
    g0 = 4 * D_A
    n_gates = 2 * N_HEADS

    w = w_in[l]
    wmain = jnp.concatenate([w[:, :g0], w[:, g0 + n_gates:]], axis=1).astype(_BF16)
    wg = jnp.pad(w[:, g0:g0 + n_gates], ((0, 0), (0, GATE_COLS - n_gates))).astype(_BF16)
    bmain = jnp.concatenate([b_in[l, :g0], b_in[l, g0 + n_gates:]])[None, :]
    bg = jnp.pad(b_in[l, g0:g0 + n_gates], (0, GATE_COLS - n_gates))[None, :]
    nm = norm_mix[l][None, :]
    convw = conv_w[l]
    convb = conv_b[l][None, :]
    mnorm = mlstm_norm[l][None, :]
    hnorm = hgrn_norm[l][None, :]
    lbl = hgrn_lb_logits.astype(_F32)
    wout = w_out[l].astype(_BF16)
    wgate = w_gate[l].astype(_BF16)
    wup = w_up[l].astype(_BF16)
    wdown = w_down[l].astype(_BF16)
    nf = norm_ffn[l][None, :]
    nfin = norm_final[None, :]

    mix_p, conv_p, C_p, n_p, m_p, S_p = _prompt_mixer(x_prompt, nm, wmain, bmain, wg, bg, convw, convb,
                                                      mnorm, lbl, hnorm)
    y_prompt = _ffn(x_prompt.reshape(B * T, D), mix_p.reshape(B * T, D_MODEL), wout, nf, wgate, wup, wdown,
                    nfin).reshape(B, T, D)

    xs_pad = jnp.pad(x_sample, ((0, 0), (0, TT_SAMPLE - TS), (0, 0)))
    p_s, g_s = _sample_proj(xs_pad.reshape(BS * TT_SAMPLE, D), nm, wmain, bmain, wg, bg)
    mix_s, conv_s, C_s, n_s, m_s, S_s = _sample_mixer(
        p_s.reshape(BS, TT_SAMPLE, P_MAIN), g_s.reshape(BS, TT_SAMPLE, GATE_COLS), state_conv[l],
        state_mlstm_C[l], state_mlstm_n[l], state_mlstm_m[l][:, :, None], state_hgrn_S[l],
        convw, convb, mnorm, lbl, hnorm, TS - 1)
    y_sample = _ffn(x_sample.reshape(BS * TS, D), mix_s[:, :TS, :].reshape(BS * TS, D_MODEL), wout, nf, wgate,
                    wup, wdown, nfin).reshape(BS, TS, D)

    return (y_prompt, y_sample,
            conv_p[None], C_p[None], n_p[None], m_p[None, :, :N_HEADS, 0], S_p[None],
            conv_s[None], C_s[None], n_s[None], m_s[None, :, :N_HEADS, 0], S_s[None])
```

```python
import functools

import jax
import jax.numpy as jnp
from jax import lax
from jax.experimental import pallas as pl
from jax.experimental.pallas import tpu as pltpu

D_MODEL = 1024
D_A = 512
D_B = 512
N_HEADS = 4
DH = 128
CONV_W = 4
D_FF = 2816
EPS = 1e-6
P_MAIN = 2 * D_A + D_A + D_A + 4 * D_B
GATE_COLS = 128
SUBLANES = 8
CONV_PAD = 8
VMEM_LIMIT = 56 * 1024 * 1024

TT_PROMPT = 256
TT_SAMPLE = 8
TM_FFN = 512
SEQS_PER_STEP = 8

_F32 = jnp.float32
_BF16 = jnp.bfloat16


def _dot_dims(a, b, ca, cb):
    batch = ((0,), (0,)) if a.ndim == 3 and b.ndim == 3 else ((), ())
    return lax.dot_general(a, b, (((a.ndim + ca,), (b.ndim + cb,)), batch), preferred_element_type=_F32)


def _dot(a, b):
    return _dot_dims(a, b, -1, -2)


def _dot_nt(a, b):
    return _dot_dims(a, b, -1, -1)


def _dot_tn(a, b):
    return _dot_dims(a, b, -2, -2)


def _sigmoid(x):
    return 1.0 / (1.0 + jnp.exp(-x))


def _rmsnorm(x, g):
    return x * lax.rsqrt(jnp.mean(x * x, axis=-1, keepdims=True) + EPS) * g


def _tril_matmul_cumsum(x, tril_bf16):
    hi = x.astype(_BF16)
    r1 = x - hi.astype(_F32)
    mid = r1.astype(_BF16)
    lo = (r1 - mid.astype(_F32)).astype(_BF16)
    return _dot(tril_bf16, hi) + _dot(tril_bf16, mid) + _dot(tril_bf16, lo)


def _sublane_cumsum(x, row):
    for sh in (1, 2, 4):
        x = x + jnp.where(row >= sh, pltpu.roll(x, sh, x.ndim - 2), 0.0)
    return x


def _row_at(x, row, t):
    return jnp.sum(jnp.where(row == t, x, 0.0), axis=-2, keepdims=True)


def _mlstm_head(q, k, v, igc, lfc, C0, n0, m0, tril, eye, row, t_last):
    lfr = jnp.sum(jnp.where(eye, lfc, 0.0), axis=-2, keepdims=True)
    Fc = jnp.sum(jnp.where(tril, lfr, 0.0), axis=-1, keepdims=True)
    uc = igc - Fc
    ur = jnp.sum(jnp.where(eye, uc, 0.0), axis=-2, keepdims=True)
    a = Fc + m0
    D = jnp.where(tril, Fc + ur, -jnp.inf)
    mt = jnp.maximum(a, jnp.max(D, axis=-1, keepdims=True))
    w_inter = jnp.exp(a - mt)
    W = jnp.exp(D - mt)
    qb = q.astype(_BF16)
    kb = k.astype(_BF16)
    vb = v.astype(_BF16)
    s = _dot_nt(qb, kb) * W
    num = w_inter * _dot(qb, C0.astype(_BF16)) + _dot(s.astype(_BF16), vb)
    den = w_inter * jnp.sum(q * n0, axis=-1, keepdims=True) + jnp.sum(s, axis=-1, keepdims=True)
    h = num / jnp.maximum(jnp.abs(den), jnp.exp(-mt))
    F_last = _row_at(Fc, row, t_last)
    m_last = _row_at(mt, row, t_last)
    decay = _row_at(w_inter, row, t_last)
    valid = row <= t_last
    wl = jnp.where(valid, jnp.exp(jnp.where(valid, F_last + uc - m_last, 0.0)), 0.0)
    kw = wl * k
    C_new = decay * C0 + _dot_tn(kw.astype(_BF16), vb)
    n_new = decay * n0 + jnp.sum(kw, axis=-2, keepdims=True)
    return h, C_new, n_new, m_last


def _hgrn_head(q_raw, f_raw, i_raw, lb, S0, tril_bf16, eye_dh, xor_rc, row, t_last, b_ref, qt_ref, kt_ref):
    tt = q_raw.shape[-2]
    ax = q_raw.ndim - 2
    f = lb + (1.0 - lb) * _sigmoid(f_raw)
    lf = jnp.log(f)
    kk = (1.0 - lb) * _sigmoid(-f_raw)
    q = q_raw * _sigmoid(q_raw)
    v = i_raw
    vb = v.astype(_BF16)
    if tt == SUBLANES:
        b = _sublane_cumsum(lf, row)
    else:
        b = _tril_matmul_cumsum(lf, tril_bf16)
    o = _dot((q * jnp.exp(b)).astype(_BF16), S0.astype(_BF16))

    fz = jnp.where((row & (SUBLANES - 1)) == 0, 0.0, f)
    o = o + jnp.sum(q * kk, axis=-1, keepdims=True) * v
    E = fz
    for d in range(1, SUBLANES):
        if d > 1:
            E = E * pltpu.roll(fz, d - 1, ax)
        A = jnp.sum(q * pltpu.roll(kk, d, ax) * E, axis=-1, keepdims=True)
        o = o + A * pltpu.roll(v, d, ax)

    if tt > SUBLANES:
        b_ref[...] = b
        A_far = None
        lvl = 0
        h = SUBLANES
        while h < tt:
            cols = slice(lvl * DH, (lvl + 1) * DH)
            for j in range(tt // h):
                rows = slice(j * h, (j + 1) * h)
                if j % 2 == 1:
                    b_mid = b_ref[pl.ds(j * h - 1, 1), :]
                    qt_ref[rows, cols] = q[rows, :] * jnp.exp(b[rows, :] - b_mid)
                    kt_ref[rows, cols] = jnp.zeros((h, DH), _F32)
                else:
                    b_mid = b_ref[pl.ds((j + 1) * h - 1, 1), :]
                    kt_ref[rows, cols] = kk[rows, :] * jnp.exp(b_mid - b[rows, :])
                    qt_ref[rows, cols] = jnp.zeros((h, DH), _F32)
            A_h = _dot_nt(qt_ref[:, cols].astype(_BF16), kt_ref[:, cols].astype(_BF16))
            if 2 * h < tt:
                A_h = jnp.where(xor_rc < 2 * h, A_h, 0.0)
            A_far = A_h if A_far is None else A_far + A_h
            lvl += 1
            h *= 2
        o = o + _dot(A_far.astype(_BF16), vb)

    bL = _row_at(b, row, t_last)
    valid = row <= t_last
    ks = jnp.where(valid, kk * jnp.exp(jnp.where(valid, bL - b, 0.0)), 0.0)
    bL_col = jnp.sum(jnp.where(eye_dh, bL, 0.0), axis=-1, keepdims=True)
    S_new = jnp.exp(bL_col) * S0 + _dot_tn(ks.astype(_BF16), vb)
    return o, S_new


def _headnorm(h, g):
    return h * lax.rsqrt(jnp.mean(h * h, axis=-1, keepdims=True) + EPS) * g


def _mixers(p_ref, g, convw_ref, convb_ref, mnorm_ref, lbl_ref, hnorm_ref,
            mix_ref, convout_ref, C_in, n_in, m_in, S_in, C_out, n_out, m_out, S_out,
            convbuf, b_ref, qt_ref, kt_ref, tt, t_last):
    lead = (slice(None),) * (len(p_ref.shape) - 2)

    def rows(start, size):
        return lead + (pl.ds(start, size), slice(None))

    def cols(start, size):
        return lead + (slice(None), slice(start, start + size))

    def head(hd):
        return lead + (hd,)

    def head_row(hd):
        return lead + (slice(hd, hd + 1), slice(None))

    row = lax.broadcasted_iota(jnp.int32, (tt, 1), 0)
    ri = lax.broadcasted_iota(jnp.int32, (tt, tt), 0)
    ci = lax.broadcasted_iota(jnp.int32, (tt, tt), 1)
    tril = ci <= ri
    eye = ci == ri
    tril_bf16 = jnp.where(tril, 1.0, 0.0).astype(_BF16)
    xor_rc = ri ^ ci
    rd = lax.broadcasted_iota(jnp.int32, (DH, DH), 0)
    cd = lax.broadcasted_iota(jnp.int32, (DH, DH), 1)
    eye_dh = rd == cd

    convbuf[rows(CONV_PAD, tt)] = p_ref[cols(0, 2 * D_A)]
    conv = convb_ref[...]
    for j in range(CONV_W):
        conv = conv + convw_ref[j:j + 1, :] * convbuf[rows(CONV_PAD - (CONV_W - 1) + j, tt)]
    new_rows = convbuf[rows(CONV_PAD + t_last - (CONV_W - 2), CONV_W - 1)]
    convout_ref[...] = new_rows
    convbuf[rows(CONV_PAD - (CONV_W - 1), CONV_W - 1)] = new_rows
    qk = conv * _sigmoid(conv)

    lf_all = jnp.minimum(g, 0.0) - jnp.log(1.0 + jnp.exp(-jnp.abs(g)))

    for hd in range(N_HEADS):
        cs = slice(hd * DH, (hd + 1) * DH)
        q = qk[..., hd * DH:(hd + 1) * DH] * (DH ** -0.5)
        k = qk[..., D_A + hd * DH:D_A + (hd + 1) * DH]
        v = p_ref[cols(2 * D_A + hd * DH, DH)]
        o_raw = p_ref[cols(3 * D_A + hd * DH, DH)]
        igc = g[..., hd:hd + 1]
        lfc = lf_all[..., N_HEADS + hd:N_HEADS + hd + 1]
        h, C_new, n_new, m_new = _mlstm_head(q, k, v, igc, lfc, C_in[head(hd)], n_in[head_row(hd)], m_in(hd),
                                             tril, eye, row, t_last)
        C_out[head(hd)] = C_new
        n_out[head_row(hd)] = n_new
        m_out[head_row(hd)] = jnp.broadcast_to(m_new, m_new.shape[:-1] + (GATE_COLS,))
        out_a = _sigmoid(o_raw) * _headnorm(h, mnorm_ref[:, cs])
        mix_ref[cols(hd * DH, DH)] = out_a.astype(mix_ref.dtype)

    logits = lbl_ref[...]
    ex = jnp.exp(logits - jnp.max(logits, axis=0, keepdims=True))
    lb_all = ex[0:1, :] / jnp.sum(ex, axis=0, keepdims=True)
    base = 4 * D_A
    for hd in range(N_HEADS):
        cs = slice(hd * DH, (hd + 1) * DH)
        q_raw = p_ref[cols(base + hd * DH, DH)]
        f_raw = p_ref[cols(base + D_B + hd * DH, DH)]
        i_raw = p_ref[cols(base + 2 * D_B + hd * DH, DH)]
        g_raw = p_ref[cols(base + 3 * D_B + hd * DH, DH)]
        o, S_new = _hgrn_head(q_raw, f_raw, i_raw, lb_all[:, cs], S_in[head(hd)], tril_bf16, eye_dh, xor_rc, row,
                              t_last, b_ref, qt_ref, kt_ref)
        S_out[head(hd)] = S_new
        out_b = _headnorm(o, hnorm_ref[:, cs]) * (g_raw * _sigmoid(g_raw))
        mix_ref[cols(D_A + hd * DH, DH)] = out_b.astype(mix_ref.dtype)


def _project(x, nm_ref, wmain_ref, bmain_ref, wg_ref, bg_ref):
    h = _rmsnorm(x, nm_ref[...]).astype(_BF16)
    p = _dot(h, wmain_ref[...]) + bmain_ref[...]
    g = _dot(h, wg_ref[...]) + bg_ref[...]
    return p, g


def _prompt_kernel(x_ref, nm_ref, wmain_ref, bmain_ref, wg_ref, bg_ref, convw_ref, convb_ref, mnorm_ref,
                   lbl_ref, hnorm_ref,
                   mix_ref, convp_ref, C_ref, n_ref, m_ref, S_ref,
                   p_scr, convbuf, b_ref, qt_ref, kt_ref):
    tt = x_ref.shape[0]

    @pl.when(pl.program_id(1) == 0)
    def _():
        C_ref[...] = jnp.zeros_like(C_ref)
        n_ref[...] = jnp.zeros_like(n_ref)
        m_ref[...] = jnp.zeros_like(m_ref)
        S_ref[...] = jnp.zeros_like(S_ref)
        convbuf[0:CONV_PAD, :] = jnp.zeros((CONV_PAD, 2 * D_A), _F32)

    p, g = _project(x_ref[...], nm_ref, wmain_ref, bmain_ref, wg_ref, bg_ref)
    p_scr[...] = p
    _mixers(p_scr, g, convw_ref, convb_ref, mnorm_ref, lbl_ref, hnorm_ref,
            mix_ref, convp_ref, C_ref, n_ref, lambda hd: m_ref[hd:hd + 1, 0:1], S_ref,
            C_ref, n_ref, m_ref, S_ref, convbuf, b_ref, qt_ref, kt_ref, tt, tt - 1)


def _proj_kernel(x_ref, nm_ref, wmain_ref, bmain_ref, wg_ref, bg_ref, p_ref, g_ref):
    p, g = _project(x_ref[...], nm_ref, wmain_ref, bmain_ref, wg_ref, bg_ref)
    p_ref[...] = p
    g_ref[...] = g


def _sample_kernel(p_ref, g_ref, conv0_ref, C0_ref, n0_ref, m0_ref, S0_ref,
                   convw_ref, convb_ref, mnorm_ref, lbl_ref, hnorm_ref,
                   mix_ref, convs_ref, C_ref, n_ref, m_ref, S_ref,
                   convbuf, t_last):
    n_seq, tt, _ = p_ref.shape
    lead_rows = CONV_PAD - (CONV_W - 1)
    convbuf[:, 0:lead_rows, :] = jnp.zeros((n_seq, lead_rows, 2 * D_A), _F32)
    convbuf[:, lead_rows:CONV_PAD, :] = conv0_ref[...]
    m_ref[...] = jnp.zeros_like(m_ref)
    _mixers(p_ref, g_ref[...], convw_ref, convb_ref, mnorm_ref, lbl_ref, hnorm_ref,
            mix_ref, convs_ref, C0_ref, n0_ref, lambda hd: m0_ref[:, hd:hd + 1, :], S0_ref,
            C_ref, n_ref, m_ref, S_ref, convbuf, None, None, None, tt, t_last)


def _ffn_kernel(x_ref, mix_ref, wout_ref, nf_ref, wgate_ref, wup_ref, wdown_ref, nfin_ref, y_ref):
    x1 = x_ref[...] + _dot(mix_ref[...], wout_ref[...])
    h2 = _rmsnorm(x1, nf_ref[...]).astype(_BF16)
    gate = _dot(h2, wgate_ref[...])
    up = _dot(h2, wup_ref[...])
    ff = (gate * _sigmoid(gate) * up).astype(_BF16)
    x2 = x1 + _dot(ff, wdown_ref[...])
    y_ref[...] = _rmsnorm(x2, nfin_ref[...])


def _resident(shape):
    nd = len(shape)
    return pl.BlockSpec(shape, lambda *_: (0,) * nd, pipeline_mode=pl.Buffered(1))


def _prompt_mixer(x, nm, wmain, bmain, wg, bg, convw, convb, mnorm, lbl, hnorm):
    B, T, D = x.shape
    tt = TT_PROMPT
    nlev = (tt // SUBLANES).bit_length() - 1
    out_shape = (
        jax.ShapeDtypeStruct((B, T, D_MODEL), _BF16),
        jax.ShapeDtypeStruct((B, CONV_W - 1, 2 * D_A), _F32),
        jax.ShapeDtypeStruct((B, N_HEADS, DH, DH), _F32),
        jax.ShapeDtypeStruct((B, N_HEADS, DH), _F32),
        jax.ShapeDtypeStruct((B, SUBLANES, GATE_COLS), _F32),
        jax.ShapeDtypeStruct((B, N_HEADS, DH, DH), _F32),
    )
    in_specs = [
        pl.BlockSpec((None, tt, D), lambda b, t: (b, t, 0)),
        _resident(nm.shape), _resident(wmain.shape), _resident(bmain.shape), _resident(wg.shape),
        _resident(bg.shape), _resident(convw.shape), _resident(convb.shape), _resident(mnorm.shape),
        _resident(lbl.shape), _resident(hnorm.shape),
    ]
    out_specs = (
        pl.BlockSpec((None, tt, D_MODEL), lambda b, t: (b, t, 0)),
        pl.BlockSpec((None, CONV_W - 1, 2 * D_A), lambda b, t: (b, 0, 0)),
        pl.BlockSpec((None, N_HEADS, DH, DH), lambda b, t: (b, 0, 0, 0)),
        pl.BlockSpec((None, N_HEADS, DH), lambda b, t: (b, 0, 0)),
        pl.BlockSpec((None, SUBLANES, GATE_COLS), lambda b, t: (b, 0, 0)),
        pl.BlockSpec((None, N_HEADS, DH, DH), lambda b, t: (b, 0, 0, 0)),
    )
    scratch = [
        pltpu.VMEM((tt, P_MAIN), _F32),
        pltpu.VMEM((CONV_PAD + tt, 2 * D_A), _F32),
        pltpu.VMEM((tt, DH), _F32),
        pltpu.VMEM((tt, nlev * DH), _F32),
        pltpu.VMEM((tt, nlev * DH), _F32),
    ]
    return pl.pallas_call(
        _prompt_kernel, out_shape=out_shape, grid=(B, T // tt), in_specs=in_specs, out_specs=out_specs,
        scratch_shapes=scratch, name="prompt_mixer",
        compiler_params=pltpu.CompilerParams(dimension_semantics=("arbitrary", "arbitrary"),
                                             vmem_limit_bytes=VMEM_LIMIT),
    )(x, nm, wmain, bmain, wg, bg, convw, convb, mnorm, lbl, hnorm)


def _sample_proj(x, nm, wmain, bmain, wg, bg):
    N, D = x.shape
    tm = min(N, 512)
    return pl.pallas_call(
        _proj_kernel,
        out_shape=(jax.ShapeDtypeStruct((N, P_MAIN), _F32), jax.ShapeDtypeStruct((N, GATE_COLS), _F32)),
        grid=(N // tm,),
        in_specs=[pl.BlockSpec((tm, D), lambda i: (i, 0)), _resident(nm.shape), _resident(wmain.shape),
                  _resident(bmain.shape), _resident(wg.shape), _resident(bg.shape)],
        out_specs=(pl.BlockSpec((tm, P_MAIN), lambda i: (i, 0)), pl.BlockSpec((tm, GATE_COLS), lambda i: (i, 0))),
        name="sample_proj",
        compiler_params=pltpu.CompilerParams(dimension_semantics=("arbitrary",), vmem_limit_bytes=VMEM_LIMIT),
    )(x, nm, wmain, bmain, wg, bg)


def _sample_mixer(p, g, conv0, C0, n0, m0, S0, convw, convb, mnorm, lbl, hnorm, t_last):
    B, tt, _ = p.shape
    out_shape = (
        jax.ShapeDtypeStruct((B, tt, D_MODEL), _BF16),
        jax.ShapeDtypeStruct((B, CONV_W - 1, 2 * D_A), _F32),
        jax.ShapeDtypeStruct((B, N_HEADS, DH, DH), _F32),
        jax.ShapeDtypeStruct((B, N_HEADS, DH), _F32),
        jax.ShapeDtypeStruct((B, SUBLANES, GATE_COLS), _F32),
        jax.ShapeDtypeStruct((B, N_HEADS, DH, DH), _F32),
    )
    ns = SEQS_PER_STEP
    in_specs = [
        pl.BlockSpec((ns, tt, P_MAIN), lambda b: (b, 0, 0)),
        pl.BlockSpec((ns, tt, GATE_COLS), lambda b: (b, 0, 0)),
        pl.BlockSpec((ns, CONV_W - 1, 2 * D_A), lambda b: (b, 0, 0)),
        pl.BlockSpec((ns, N_HEADS, DH, DH), lambda b: (b, 0, 0, 0)),
        pl.BlockSpec((ns, N_HEADS, DH), lambda b: (b, 0, 0)),
        pl.BlockSpec((ns, N_HEADS, 1), lambda b: (b, 0, 0)),
        pl.BlockSpec((ns, N_HEADS, DH, DH), lambda b: (b, 0, 0, 0)),
        _resident(convw.shape), _resident(convb.shape), _resident(mnorm.shape), _resident(lbl.shape),
        _resident(hnorm.shape),
    ]
    out_specs = (
        pl.BlockSpec((ns, tt, D_MODEL), lambda b: (b, 0, 0)),
        pl.BlockSpec((ns, CONV_W - 1, 2 * D_A), lambda b: (b, 0, 0)),
        pl.BlockSpec((ns, N_HEADS, DH, DH), lambda b: (b, 0, 0, 0)),
        pl.BlockSpec((ns, N_HEADS, DH), lambda b: (b, 0, 0)),
        pl.BlockSpec((ns, SUBLANES, GATE_COLS), lambda b: (b, 0, 0)),
        pl.BlockSpec((ns, N_HEADS, DH, DH), lambda b: (b, 0, 0, 0)),
    )
    return pl.pallas_call(
        functools.partial(_sample_kernel, t_last=t_last),
        out_shape=out_shape, grid=(B // ns,), in_specs=in_specs, out_specs=out_specs,
        scratch_shapes=[pltpu.VMEM((ns, CONV_PAD + tt, 2 * D_A), _F32)], name="sample_mixer",
        compiler_params=pltpu.CompilerParams(dimension_semantics=("arbitrary",), vmem_limit_bytes=VMEM_LIMIT),
    )(p, g, conv0, C0, n0, m0, S0, convw, convb, mnorm, lbl, hnorm)


def _ffn(x, mix, wout, nf, wgate, wup, wdown, nfin):
    N, D = x.shape
    tm = min(N, TM_FFN)
    return pl.pallas_call(
        _ffn_kernel, out_shape=jax.ShapeDtypeStruct((N, D), _F32), grid=(N // tm,),
        in_specs=[pl.BlockSpec((tm, D), lambda i: (i, 0)), pl.BlockSpec((tm, D_MODEL), lambda i: (i, 0)),
                  _resident(wout.shape), _resident(nf.shape), _resident(wgate.shape), _resident(wup.shape),
                  _resident(wdown.shape), _resident(nfin.shape)],
        out_specs=pl.BlockSpec((tm, D), lambda i: (i, 0)), name="ffn",
        compiler_params=pltpu.CompilerParams(dimension_semantics=("arbitrary",), vmem_limit_bytes=VMEM_LIMIT),
    )(x, mix, wout, nf, wgate, wup, wdown, nfin)


def kernel(x_prompt, x_sample, state_conv, state_mlstm_C, state_mlstm_n, state_mlstm_m, state_hgrn_S,
           norm_mix, w_in, b_in, conv_w, conv_b, mlstm_norm, hgrn_lb_logits, hgrn_norm, w_out,
           norm_ffn, w_gate, w_up, w_down, norm_final):
    depth = w_in.shape[0]
    assert depth == 1, "single-layer stack"
    B, T, D = x_prompt.shape
    BS, TS, _ = x_sample.shape
    l = 0
    g0 = 4 * D_A
    n_gates = 2 * N_HEADS

    w = w_in[l]
    wmain = jnp.concatenate([w[:, :g0], w[:, g0 + n_gates:]], axis=1).astype(_BF16)
    wg = jnp.pad(w[:, g0:g0 + n_gates], ((0, 0), (0, GATE_COLS - n_gates))).astype(_BF16)
    bmain = jnp.concatenate([b_in[l, :g0], b_in[l, g0 + n_gates:]])[None, :]
    bg = jnp.pad(b_in[l, g0:g0 + n_gates], (0, GATE_COLS - n_gates))[None, :]
    nm = norm_mix[l][None, :]
    convw = conv_w[l]
    convb = conv_b[l][None, :]
    mnorm = mlstm_norm[l][None, :]
    hnorm = hgrn_norm[l][None, :]
    lbl = hgrn_lb_logits.astype(_F32)
    wout = w_out[l].astype(_BF16)
    wgate = w_gate[l].astype(_BF16)
    wup = w_up[l].astype(_BF16)
    wdown = w_down[l].astype(_BF16)
    nf = norm_ffn[l][None, :]
    nfin = norm_final[None, :]

    mix_p, conv_p, C_p, n_p, m_p, S_p = _prompt_mixer(x_prompt, nm, wmain, bmain, wg, bg, convw, convb,
                                                      mnorm, lbl, hnorm)
    y_prompt = _ffn(x_prompt.reshape(B * T, D), mix_p.reshape(B * T, D_MODEL), wout, nf, wgate, wup, wdown,
                    nfin).reshape(B, T, D)

    xs_pad = jnp.pad(x_sample, ((0, 0), (0, TT_SAMPLE - TS), (0, 0)))
    p_s, g_s = _sample_proj(xs_pad.reshape(BS * TT_SAMPLE, D), nm, wmain, bmain, wg, bg)
    mix_s, conv_s, C_s, n_s, m_s, S_s = _sample_mixer(
        p_s.reshape(BS, TT_SAMPLE, P_MAIN), g_s.reshape(BS, TT_SAMPLE, GATE_COLS), state_conv[l],
        state_mlstm_C[l], state_mlstm_n[l], state_mlstm_m[l][:, :, None], state_hgrn_S[l],
        convw, convb, mnorm, lbl, hnorm, TS - 1)
    y_sample = _ffn(x_sample.reshape(BS * TS, D), mix_s[:, :TS, :].reshape(BS * TS, D_MODEL), wout, nf, wgate,
                    wup, wdown, nfin).reshape(BS, TS, D)

    return (y_prompt, y_sample,
            conv_p[None], C_p[None], n_p[None], m_p[None, :, :N_HEADS, 0], S_p[None],
            conv_s[None], C_s[None], n_s[None], m_s[None, :, :N_HEADS, 0], S_s[None])
```

```python
import functools

import jax
import jax.numpy as jnp
from jax import lax
from jax.experimental import pallas as pl
from jax.experimental.pallas import tpu as pltpu

D_MODEL = 1024
D_A = 512
D_B = 512
N_HEADS = 4
DH = 128
CONV_W = 4
D_FF = 2816
EPS = 1e-6
P_MAIN = 2 * D_A + D_A + D_A + 4 * D_B
GATE_COLS = 128
SUBLANES = 8
CONV_PAD = 8
VMEM_LIMIT = 56 * 1024 * 1024

TT_PROMPT = 256
CHUNK_PROMPT = 256
TT_SAMPLE = 8
TM_FFN = 512
SEQS_PER_STEP = 8

_F32 = jnp.float32
_BF16 = jnp.bfloat16


def _dot_dims(a, b, ca, cb):
    batch = ((0,), (0,)) if a.ndim == 3 and b.ndim == 3 else ((), ())
    return lax.dot_general(a, b, (((a.ndim + ca,), (b.ndim + cb,)), batch), preferred_element_type=_F32)


def _dot(a, b):
    return _dot_dims(a, b, -1, -2)


def _dot_nt(a, b):
    return _dot_dims(a, b, -1, -1)


def _dot_tn(a, b):
    return _dot_dims(a, b, -2, -2)


def _sigmoid(x):
    return 1.0 / (1.0 + jnp.exp(-x))


def _rmsnorm(x, g):
    return x * lax.rsqrt(jnp.mean(x * x, axis=-1, keepdims=True) + EPS) * g


def _bf16_pieces(x):
    hi = x.astype(_BF16)
    r1 = x - hi.astype(_F32)
    mid = r1.astype(_BF16)
    lo = (r1 - mid.astype(_F32)).astype(_BF16)
    return hi, mid, lo


def _tril_matmul_cumsum(x, tril_bf16):
    hi, mid, lo = _bf16_pieces(x)
    return _dot(tril_bf16, hi) + _dot(tril_bf16, mid) + _dot(tril_bf16, lo)


def _triu_matmul_cumsum(x, triu_bf16):
    hi, mid, lo = _bf16_pieces(x)
    return _dot(hi, triu_bf16) + _dot(mid, triu_bf16) + _dot(lo, triu_bf16)


def _log_sigmoid(x):
    return jnp.minimum(x, 0.0) - jnp.log(1.0 + jnp.exp(-jnp.abs(x)))


def _sublane_cumsum(x, row):
    for sh in (1, 2, 4):
        x = x + jnp.where(row >= sh, pltpu.roll(x, sh, x.ndim - 2), 0.0)
    return x


def _row_at(x, row, t):
    return jnp.sum(jnp.where(row == t, x, 0.0), axis=-2, keepdims=True)


def _gate_vectors_small(igc, lfc, tril, eye):
    lfr = jnp.sum(jnp.where(eye, lfc, 0.0), axis=-2, keepdims=True)
    Fc = jnp.sum(jnp.where(tril, lfr, 0.0), axis=-1, keepdims=True)
    uc = igc - Fc
    ur = jnp.sum(jnp.where(eye, uc, 0.0), axis=-2, keepdims=True)
    return Fc, uc, ur


def _mlstm_head(q, k, v, Fc, uc, ur, C0, n0, m0, tril, row, t_last):
    a = Fc + m0
    D = jnp.where(tril, Fc + ur, -jnp.inf)
    mt = jnp.maximum(a, jnp.max(D, axis=-1, keepdims=True))
    w_inter = jnp.exp(a - mt)
    W = jnp.exp(D - mt)
    qb = q.astype(_BF16)
    kb = k.astype(_BF16)
    vb = v.astype(_BF16)
    s = _dot_nt(qb, kb) * W
    num = w_inter * _dot(qb, C0.astype(_BF16)) + _dot(s.astype(_BF16), vb)
    den = w_inter * jnp.sum(q * n0, axis=-1, keepdims=True) + jnp.sum(s, axis=-1, keepdims=True)
    h = num / jnp.maximum(jnp.abs(den), jnp.exp(-mt))
    F_last = _row_at(Fc, row, t_last)
    m_last = _row_at(mt, row, t_last)
    decay = _row_at(w_inter, row, t_last)
    valid = row <= t_last
    wl = jnp.where(valid, jnp.exp(jnp.where(valid, F_last + uc - m_last, 0.0)), 0.0)
    kw = wl * k
    C_new = decay * C0 + _dot_tn(kw.astype(_BF16), vb)
    n_new = decay * n0 + jnp.sum(kw, axis=-2, keepdims=True)
    return h, C_new, n_new, m_last


def _hgrn_core(q, kk, v, f, b, S0, eye_dh, xor_rc, row, t_last, b_ref, qt_ref, kt_ref, nd_ref):
    n_batch, tt, _ = q.shape
    vb = v.astype(_BF16)
    o = _dot((q * jnp.exp(b)).astype(_BF16), S0.astype(_BF16))

    if nd_ref is None:
        fz = jnp.where((row & (SUBLANES - 1)) == 0, 0.0, f)
        o = o + jnp.sum(q * kk, axis=-1, keepdims=True) * v
        E = fz
        for d in range(1, SUBLANES):
            if d > 1:
                E = E * pltpu.roll(fz, d - 1, 1)
            A = jnp.sum(q * pltpu.roll(kk, d, 1) * E, axis=-1, keepdims=True)
            o = o + A * pltpu.roll(v, d, 1)
    else:
        for idx, val in enumerate((q, kk, v, f)):
            nd_ref[idx] = val
        group = SUBLANES * SUBLANES
        for n in range(n_batch):
            for g0 in range(0, tt, group):
                views = [[nd_ref[idx, n, pl.ds(g0 + pos, SUBLANES, stride=SUBLANES), :]
                          for pos in range(SUBLANES)] for idx in range(4)]
                qv, kv, vv, fv = views
                for tp in range(SUBLANES):
                    acc = jnp.sum(qv[tp] * kv[tp], axis=-1, keepdims=True) * vv[tp]
                    E = None
                    for sp in range(tp - 1, -1, -1):
                        E = fv[sp + 1] if E is None else E * fv[sp + 1]
                        A = jnp.sum(qv[tp] * kv[sp] * E, axis=-1, keepdims=True)
                        acc = acc + A * vv[sp]
                    nd_ref[4, n, pl.ds(g0 + tp, SUBLANES, stride=SUBLANES), :] = acc
        o = o + nd_ref[4]

    if tt > SUBLANES:
        b_ref[...] = b
        A_far = None
        lvl = 0
        h = SUBLANES
        while h < tt:
            cols = slice(lvl * DH, (lvl + 1) * DH)
            for j in range(tt // h):
                rows = slice(j * h, (j + 1) * h)
                if j % 2 == 1:
                    b_mid = b_ref[:, pl.ds(j * h - 1, 1), :]
                    qt_ref[:, rows, cols] = q[:, rows, :] * jnp.exp(b[:, rows, :] - b_mid)
                    kt_ref[:, rows, cols] = jnp.zeros((n_batch, h, DH), _F32)
                else:
                    b_mid = b_ref[:, pl.ds((j + 1) * h - 1, 1), :]
                    kt_ref[:, rows, cols] = kk[:, rows, :] * jnp.exp(b_mid - b[:, rows, :])
                    qt_ref[:, rows, cols] = jnp.zeros((n_batch, h, DH), _F32)
            A_h = _dot_nt(qt_ref[:, :, cols].astype(_BF16), kt_ref[:, :, cols].astype(_BF16))
            if 2 * h < tt:
                A_h = jnp.where(xor_rc < 2 * h, A_h, 0.0)
            A_far = A_h if A_far is None else A_far + A_h
            lvl += 1
            h *= 2
        o = o + _dot(A_far.astype(_BF16), vb)

    bL = _row_at(b, row, t_last)
    valid = row <= t_last
    ks = jnp.where(valid, kk * jnp.exp(jnp.where(valid, bL - b, 0.0)), 0.0)
    bL_col = jnp.sum(jnp.where(eye_dh, bL, 0.0), axis=-1, keepdims=True)
    S_new = jnp.exp(bL_col) * S0 + _dot_tn(ks.astype(_BF16), vb)
    return o, S_new


def _headnorm(h, g):
    return h * lax.rsqrt(jnp.mean(h * h, axis=-1, keepdims=True) + EPS) * g


def _mixers(p_ref, g, convw_ref, convb_ref, mnorm_ref, lbl_ref, hnorm_ref,
            mix_ref, convout_ref, C_in, n_in, m_in, S_in, C_out, n_out, m_out, S_out,
            convbuf, qk_ref, b_ref, qt_ref, kt_ref, nd_ref, tt, t_last):
    batched = len(p_ref.shape) == 3
    n_tiles = p_ref.shape[0] if batched else 1
    lead = (slice(None),) if batched else ()

    def rows(start, size):
        return lead + (pl.ds(start, size), slice(None))

    def cols(start, size):
        return lead + (slice(None), slice(start, start + size))

    def head(hd):
        return lead + (hd,)

    def head_row(hd):
        return lead + (slice(hd, hd + 1), slice(None))

    def cat_heads(get):
        parts = [get(hd) for hd in range(N_HEADS)]
        return jnp.concatenate([x if batched else x[None] for x in parts], axis=0)

    def head_cols(x):
        return cat_heads(lambda hd: x[..., hd * DH:(hd + 1) * DH])

    def head_gain(ref):
        return cat_heads(lambda hd: jnp.broadcast_to(ref[:, hd * DH:(hd + 1) * DH], (n_tiles, 1, DH))
                         if batched else ref[:, hd * DH:(hd + 1) * DH])

    def of_head(x, hd):
        return x[hd * n_tiles:(hd + 1) * n_tiles] if batched else x[hd]

    row = lax.broadcasted_iota(jnp.int32, (tt, 1), 0)
    ri = lax.broadcasted_iota(jnp.int32, (tt, tt), 0)
    ci = lax.broadcasted_iota(jnp.int32, (tt, tt), 1)
    tril = ci <= ri
    eye = ci == ri
    xor_rc = ri ^ ci
    rd = lax.broadcasted_iota(jnp.int32, (DH, DH), 0)
    cd = lax.broadcasted_iota(jnp.int32, (DH, DH), 1)
    eye_dh = rd == cd

    prev = CONV_PAD - (CONV_W - 1)
    last = CONV_PAD + t_last - (CONV_W - 2)
    if qk_ref is None:
        convbuf[rows(CONV_PAD, tt)] = p_ref[cols(0, 2 * D_A)]
        conv = convb_ref[...]
        for j in range(CONV_W):
            conv = conv + convw_ref[j:j + 1, :] * convbuf[rows(prev + j, tt)]
        new_rows = convbuf[rows(last, CONV_W - 1)]
        convout_ref[...] = new_rows
        convbuf[rows(prev, CONV_W - 1)] = new_rows
        qk = conv * _sigmoid(conv)
        q = head_cols(qk[..., 0:D_A]) * (DH ** -0.5)
        k = head_cols(qk[..., D_A:2 * D_A])
    else:
        group = SUBLANES * SUBLANES
        for c in range(2 * N_HEADS):
            cs = slice(c * DH, (c + 1) * DH)
            convbuf[c, pl.ds(CONV_PAD, tt), :] = p_ref[:, cs]
            taps = [jnp.broadcast_to(convw_ref[j:j + 1, cs], (SUBLANES, DH)) for j in range(CONV_W)]
            bias = jnp.broadcast_to(convb_ref[:, cs], (SUBLANES, DH))
            for g0 in range(0, tt, group):
                views = [convbuf[c, pl.ds(g0 + prev + r, SUBLANES, stride=SUBLANES), :]
                         for r in range(SUBLANES + CONV_W - 1)]
                for pos in range(SUBLANES):
                    acc = bias
                    for j in range(CONV_W):
                        acc = acc + taps[j] * views[pos + j]
                    qk_ref[c, pl.ds(g0 + pos, SUBLANES, stride=SUBLANES), :] = acc * _sigmoid(acc)
            new_rows = convbuf[c, pl.ds(last, CONV_W - 1), :]
            convout_ref[:, cs] = new_rows
            convbuf[c, pl.ds(prev, CONV_W - 1), :] = new_rows
        q = qk_ref[0:N_HEADS] * (DH ** -0.5)
        k = qk_ref[N_HEADS:2 * N_HEADS]

    base = 4 * D_A
    f_raw = p_ref[cols(base + D_B, D_B)]
    logits = lbl_ref[...]
    ex = jnp.exp(logits - jnp.max(logits, axis=0, keepdims=True))
    lb = ex[0:1, :] / jnp.sum(ex, axis=0, keepdims=True)
    f = lb + (1.0 - lb) * _sigmoid(f_raw)
    lf = jnp.log(f)

    lf_all = _log_sigmoid(g)
    if batched or tt == SUBLANES:
        b = _sublane_cumsum(lf, row)
        Fc, uc, ur = _gate_vectors_small(cat_heads(lambda hd: g[..., hd:hd + 1]),
                                         cat_heads(lambda hd: lf_all[..., N_HEADS + hd:N_HEADS + hd + 1]),
                                         tril, eye)
    else:
        tril_bf16 = jnp.where(tril, 1.0, 0.0).astype(_BF16)
        triu_bf16 = jnp.where(ci >= ri, 1.0, 0.0).astype(_BF16)
        cum = _tril_matmul_cumsum(jnp.concatenate([lf, lf_all], axis=1), tril_bf16)
        b = cum[:, 0:D_B]
        g_t = g.T
        F_rows = _triu_matmul_cumsum(_log_sigmoid(g_t[0:2 * N_HEADS, :]), triu_bf16)
        Fc = cat_heads(lambda hd: cum[:, D_B + N_HEADS + hd:D_B + N_HEADS + hd + 1])
        uc = cat_heads(lambda hd: g[:, hd:hd + 1]) - Fc
        ur = cat_heads(lambda hd: g_t[hd:hd + 1, :] - F_rows[N_HEADS + hd:N_HEADS + hd + 1, :])

    v = head_cols(p_ref[cols(2 * D_A, D_A)])
    o_raw = head_cols(p_ref[cols(3 * D_A, D_A)])
    C0 = cat_heads(lambda hd: C_in[head(hd)])
    n0 = cat_heads(lambda hd: n_in[head_row(hd)])
    m0 = cat_heads(m_in)
    h, C_new, n_new, m_new = _mlstm_head(q, k, v, Fc, uc, ur, C0, n0, m0, tril, row, t_last)
    out_a = _sigmoid(o_raw) * _headnorm(h, head_gain(mnorm_ref))
    m_new = jnp.broadcast_to(m_new, m_new.shape[:-1] + (GATE_COLS,))
    for hd in range(N_HEADS):
        C_out[head(hd)] = of_head(C_new, hd)
        n_out[head_row(hd)] = of_head(n_new, hd)
        m_out[head_row(hd)] = of_head(m_new, hd)
        mix_ref[cols(hd * DH, DH)] = of_head(out_a, hd).astype(mix_ref.dtype)

    q_raw = p_ref[cols(base, D_B)]
    g_raw = p_ref[cols(base + 3 * D_B, D_B)]
    kk = (1.0 - lb) * _sigmoid(-f_raw)
    qs = q_raw * _sigmoid(q_raw)
    S0 = cat_heads(lambda hd: S_in[head(hd)])
    o, S_new = _hgrn_core(head_cols(qs), head_cols(kk), head_cols(p_ref[cols(base + 2 * D_B, D_B)]),
                          head_cols(f), head_cols(b), S0, eye_dh, xor_rc, row, t_last,
                          b_ref, qt_ref, kt_ref, nd_ref)
    out_b = _headnorm(o, head_gain(hnorm_ref)) * head_cols(g_raw * _sigmoid(g_raw))
    for hd in range(N_HEADS):
        S_out[head(hd)] = of_head(S_new, hd)
        mix_ref[cols(D_A + hd * DH, DH)] = of_head(out_b, hd).astype(mix_ref.dtype)


def _project(x, nm_ref, wmain_ref, bmain_ref, wg_ref, bg_ref):
    h = _rmsnorm(x, nm_ref[...]).astype(_BF16)
    half = P_MAIN // 2
    p = jnp.concatenate([_dot(h, wmain_ref[0]) + bmain_ref[:, 0:half],
                         _dot(h, wmain_ref[1]) + bmain_ref[:, half:P_MAIN]], axis=1)
    g = _dot(h, wg_ref[...]) + bg_ref[...]
    return p, g


def _prompt_kernel(x0_ref, xa_ref, xb_ref, nm_ref, wmain_ref, bmain_ref, wg_ref, bg_ref, convw_ref, convb_ref,
                   mnorm_ref, lbl_ref, hnorm_ref,
                   mix_ref, convp_ref, C_ref, n_ref, m_ref, S_ref,
                   p_a, g_a, p_b, g_b, convbuf, qk_ref, b_ref, qt_ref, kt_ref, nd_ref, steps_per_row):
    tt = xa_ref.shape[0]
    j = pl.program_id(0)

    def project_into(x_ref, p_scr, g_scr):
        p, g = _project(x_ref[...], nm_ref, wmain_ref, bmain_ref, wg_ref, bg_ref)
        p_scr[...] = p
        g_scr[...] = g

    def mix(p_scr, g_scr, half):
        ck = CHUNK_PROMPT
        for r0 in range(0, tt, ck):
            _mixers(p_scr.at[pl.ds(r0, ck)], g_scr[pl.ds(r0, ck), :], convw_ref, convb_ref, mnorm_ref, lbl_ref,
                    hnorm_ref, mix_ref.at[pl.ds(half * tt + r0, ck)], convp_ref, C_ref, n_ref,
                    lambda hd: m_ref[hd:hd + 1, 0:1], S_ref, C_ref, n_ref, m_ref, S_ref,
                    convbuf, qk_ref, b_ref, qt_ref, kt_ref, nd_ref, ck, ck - 1)

    @pl.when(j == 0)
    def _():
        project_into(x0_ref, p_a, g_a)

    @pl.when(j % steps_per_row == 0)
    def _():
        C_ref[...] = jnp.zeros_like(C_ref)
        n_ref[...] = jnp.zeros_like(n_ref)
        m_ref[...] = jnp.zeros_like(m_ref)
        S_ref[...] = jnp.zeros_like(S_ref)
        convbuf[:, 0:CONV_PAD, :] = jnp.zeros((2 * N_HEADS, CONV_PAD, DH), _F32)

    project_into(xa_ref, p_b, g_b)
    mix(p_a, g_a, 0)
    project_into(xb_ref, p_a, g_a)
    mix(p_b, g_b, 1)


def _proj_kernel(x_ref, nm_ref, wmain_ref, bmain_ref, wg_ref, bg_ref, p_ref, g_ref):
    p, g = _project(x_ref[...], nm_ref, wmain_ref, bmain_ref, wg_ref, bg_ref)
    p_ref[...] = p
    g_ref[...] = g


def _sample_kernel(p_ref, g_ref, conv0_ref, C0_ref, n0_ref, m0_ref, S0_ref,
                   convw_ref, convb_ref, mnorm_ref, lbl_ref, hnorm_ref,
                   mix_ref, convs_ref, C_ref, n_ref, m_ref, S_ref,
                   convbuf, t_last):
    n_seq, tt, _ = p_ref.shape
    lead_rows = CONV_PAD - (CONV_W - 1)
    convbuf[:, 0:lead_rows, :] = jnp.zeros((n_seq, lead_rows, 2 * D_A), _F32)
    convbuf[:, lead_rows:CONV_PAD, :] = conv0_ref[...]
    m_ref[...] = jnp.zeros_like(m_ref)
    _mixers(p_ref, g_ref[...], convw_ref, convb_ref, mnorm_ref, lbl_ref, hnorm_ref,
            mix_ref, convs_ref, C0_ref, n0_ref, lambda hd: m0_ref[:, hd:hd + 1, :], S0_ref,
            C_ref, n_ref, m_ref, S_ref, convbuf, None, None, None, None, None, tt, t_last)


def _ffn_kernel(x_ref, mix_ref, wout_ref, nf_ref, wgate_ref, wup_ref, wdown_ref, nfin_ref, y_ref):
    x1 = x_ref[...] + _dot(mix_ref[...], wout_ref[...])
    h2 = _rmsnorm(x1, nf_ref[...]).astype(_BF16)
    gate = _dot(h2, wgate_ref[...])
    up = _dot(h2, wup_ref[...])
    ff = (gate * _sigmoid(gate) * up).astype(_BF16)
    x2 = x1 + _dot(ff, wdown_ref[...])
    y_ref[...] = _rmsnorm(x2, nfin_ref[...])


def _resident(shape):
    nd = len(shape)
    return pl.BlockSpec(shape, lambda *_: (0,) * nd, pipeline_mode=pl.Buffered(1))


def _prompt_mixer(x, nm, wmain, bmain, wg, bg, convw, convb, mnorm, lbl, hnorm):
    B, T, D = x.shape
    tt = TT_PROMPT
    ck = CHUNK_PROMPT
    nlev = (ck // SUBLANES).bit_length() - 1
    n_tiles = B * T // tt
    steps_per_row = T // (2 * tt)
    x2 = x.reshape(B * T, D)
    out_shape = (
        jax.ShapeDtypeStruct((B * T, D_MODEL), _BF16),
        jax.ShapeDtypeStruct((B, CONV_W - 1, 2 * D_A), _F32),
        jax.ShapeDtypeStruct((B, N_HEADS, DH, DH), _F32),
        jax.ShapeDtypeStruct((B, N_HEADS, DH), _F32),
        jax.ShapeDtypeStruct((B, SUBLANES, GATE_COLS), _F32),
        jax.ShapeDtypeStruct((B, N_HEADS, DH, DH), _F32),
    )
    weights = (nm, wmain, bmain, wg, bg, convw, convb, mnorm, lbl, hnorm)
    in_specs = [
        pl.BlockSpec((tt, D), lambda j: (0, 0)),
        pl.BlockSpec((tt, D), lambda j: (2 * j + 1, 0)),
        pl.BlockSpec((tt, D), lambda j: (jnp.minimum(2 * j + 2, n_tiles - 1), 0)),
    ] + [_resident(w.shape) for w in weights]
    out_specs = (
        pl.BlockSpec((2 * tt, D_MODEL), lambda j: (j, 0)),
        pl.BlockSpec((None, CONV_W - 1, 2 * D_A), lambda j: (j // steps_per_row, 0, 0)),
        pl.BlockSpec((None, N_HEADS, DH, DH), lambda j: (j // steps_per_row, 0, 0, 0)),
        pl.BlockSpec((None, N_HEADS, DH), lambda j: (j // steps_per_row, 0, 0)),
        pl.BlockSpec((None, SUBLANES, GATE_COLS), lambda j: (j // steps_per_row, 0, 0)),
        pl.BlockSpec((None, N_HEADS, DH, DH), lambda j: (j // steps_per_row, 0, 0, 0)),
    )
    scratch = [
        pltpu.VMEM((tt, P_MAIN), _F32), pltpu.VMEM((tt, GATE_COLS), _F32),
        pltpu.VMEM((tt, P_MAIN), _F32), pltpu.VMEM((tt, GATE_COLS), _F32),
        pltpu.VMEM((2 * N_HEADS, CONV_PAD + ck, DH), _F32),
        pltpu.VMEM((2 * N_HEADS, ck, DH), _F32),
        pltpu.VMEM((N_HEADS, ck, DH), _F32),
        pltpu.VMEM((N_HEADS, ck, nlev * DH), _F32),
        pltpu.VMEM((N_HEADS, ck, nlev * DH), _F32),
        pltpu.VMEM((5, N_HEADS, ck, DH), _F32),
    ]
    return pl.pallas_call(
        functools.partial(_prompt_kernel, steps_per_row=steps_per_row),
        out_shape=out_shape, grid=(n_tiles // 2,), in_specs=in_specs, out_specs=out_specs,
        scratch_shapes=scratch, name="prompt_mixer",
        compiler_params=pltpu.CompilerParams(dimension_semantics=("arbitrary",), vmem_limit_bytes=VMEM_LIMIT),
    )(x2, x2, x2, *weights)


def _sample_proj(x, nm, wmain, bmain, wg, bg):
    N, D = x.shape
    tm = min(N, 512)
    return pl.pallas_call(
        _proj_kernel,
        out_shape=(jax.ShapeDtypeStruct((N, P_MAIN), _F32), jax.ShapeDtypeStruct((N, GATE_COLS), _F32)),
        grid=(N // tm,),
        in_specs=[pl.BlockSpec((tm, D), lambda i: (i, 0)), _resident(nm.shape), _resident(wmain.shape),
                  _resident(bmain.shape), _resident(wg.shape), _resident(bg.shape)],
        out_specs=(pl.BlockSpec((tm, P_MAIN), lambda i: (i, 0)), pl.BlockSpec((tm, GATE_COLS), lambda i: (i, 0))),
        name="sample_proj",
        compiler_params=pltpu.CompilerParams(dimension_semantics=("arbitrary",), vmem_limit_bytes=VMEM_LIMIT),
    )(x, nm, wmain, bmain, wg, bg)


def _sample_mixer(p, g, conv0, C0, n0, m0, S0, convw, convb, mnorm, lbl, hnorm, t_last):
    B, tt, _ = p.shape
    out_shape = (
        jax.ShapeDtypeStruct((B, tt, D_MODEL), _BF16),
        jax.ShapeDtypeStruct((B, CONV_W - 1, 2 * D_A), _F32),
        jax.ShapeDtypeStruct((B, N_HEADS, DH, DH), _F32),
        jax.ShapeDtypeStruct((B, N_HEADS, DH), _F32),
        jax.ShapeDtypeStruct((B, SUBLANES, GATE_COLS), _F32),
        jax.ShapeDtypeStruct((B, N_HEADS, DH, DH), _F32),
    )
    ns = SEQS_PER_STEP
    in_specs = [
        pl.BlockSpec((ns, tt, P_MAIN), lambda b: (b, 0, 0)),
        pl.BlockSpec((ns, tt, GATE_COLS), lambda b: (b, 0, 0)),
        pl.BlockSpec((ns, CONV_W - 1, 2 * D_A), lambda b: (b, 0, 0)),
        pl.BlockSpec((ns, N_HEADS, DH, DH), lambda b: (b, 0, 0, 0)),
        pl.BlockSpec((ns, N_HEADS, DH), lambda b: (b, 0, 0)),
        pl.BlockSpec((ns, N_HEADS, 1), lambda b: (b, 0, 0)),
        pl.BlockSpec((ns, N_HEADS, DH, DH), lambda b: (b, 0, 0, 0)),
        _resident(convw.shape), _resident(convb.shape), _resident(mnorm.shape), _resident(lbl.shape),
        _resident(hnorm.shape),
    ]
    out_specs = (
        pl.BlockSpec((ns, tt, D_MODEL), lambda b: (b, 0, 0)),
        pl.BlockSpec((ns, CONV_W - 1, 2 * D_A), lambda b: (b, 0, 0)),
        pl.BlockSpec((ns, N_HEADS, DH, DH), lambda b: (b, 0, 0, 0)),
        pl.BlockSpec((ns, N_HEADS, DH), lambda b: (b, 0, 0)),
        pl.BlockSpec((ns, SUBLANES, GATE_COLS), lambda b: (b, 0, 0)),
        pl.BlockSpec((ns, N_HEADS, DH, DH), lambda b: (b, 0, 0, 0)),
    )
    return pl.pallas_call(
        functools.partial(_sample_kernel, t_last=t_last),
        out_shape=out_shape, grid=(B // ns,), in_specs=in_specs, out_specs=out_specs,
        scratch_shapes=[pltpu.VMEM((ns, CONV_PAD + tt, 2 * D_A), _F32)], name="sample_mixer",
        compiler_params=pltpu.CompilerParams(dimension_semantics=("arbitrary",), vmem_limit_bytes=VMEM_LIMIT),
    )(p, g, conv0, C0, n0, m0, S0, convw, convb, mnorm, lbl, hnorm)


def _ffn(x, mix, wout, nf, wgate, wup, wdown, nfin):
    N, D = x.shape
    tm = min(N, TM_FFN)
    return pl.pallas_call(
        _ffn_kernel, out_shape=jax.ShapeDtypeStruct((N, D), _F32), grid=(N // tm,),
        in_specs=[pl.BlockSpec((tm, D), lambda i: (i, 0)), pl.BlockSpec((tm, D_MODEL), lambda i: (i, 0)),
                  _resident(wout.shape), _resident(nf.shape), _resident(wgate.shape), _resident(wup.shape),
                  _resident(wdown.shape), _resident(nfin.shape)],
        out_specs=pl.BlockSpec((tm, D), lambda i: (i, 0)), name="ffn",
        compiler_params=pltpu.CompilerParams(dimension_semantics=("arbitrary",), vmem_limit_bytes=VMEM_LIMIT),
    )(x, mix, wout, nf, wgate, wup, wdown, nfin)


def kernel(x_prompt, x_sample, state_conv, state_mlstm_C, state_mlstm_n, state_mlstm_m, state_hgrn_S,
           norm_mix, w_in, b_in, conv_w, conv_b, mlstm_norm, hgrn_lb_logits, hgrn_norm, w_out,
           norm_ffn, w_gate, w_up, w_down, norm_final):
    depth = w_in.shape[0]
    assert depth == 1, "single-layer stack"
    B, T, D = x_prompt.shape
    BS, TS, _ = x_sample.shape
    l = 0
    g0 = 4 * D_A
    n_gates = 2 * N_HEADS

    w = w_in[l]
    wmain = jnp.stack([w[:, :g0].astype(_BF16), w[:, g0 + n_gates:].astype(_BF16)])
    wg = jnp.pad(w[:, g0:g0 + n_gates], ((0, 0), (0, GATE_COLS - n_gates))).astype(_BF16)
    bmain = jnp.concatenate([b_in[l, :g0], b_in[l, g0 + n_gates:]])[None, :]
    bg = jnp.pad(b_in[l, g0:g0 + n_gates], (0, GATE_COLS - n_gates))[None, :]
    nm = norm_mix[l][None, :]
    convw = conv_w[l]
    convb = conv_b[l][None, :]
    mnorm = mlstm_norm[l][None, :]
    hnorm = hgrn_norm[l][None, :]
    lbl = hgrn_lb_logits.astype(_F32)
    wout = w_out[l].astype(_BF16)
    wgate = w_gate[l].astype(_BF16)
    wup = w_up[l].astype(_BF16)
    wdown = w_down[l].astype(_BF16)
    nf = norm_ffn[l][None, :]
    nfin = norm_final[None, :]

    mix_p, conv_p, C_p, n_p, m_p, S_p = _prompt_mixer(x_prompt, nm, wmain, bmain, wg, bg, convw, convb,
                                                      mnorm, lbl, hnorm)
    y_prompt = _ffn(x_prompt.reshape(B * T, D), mix_p, wout, nf, wgate, wup, wdown, nfin).reshape(B, T, D)

    xs_pad = jnp.pad(x_sample, ((0, 0), (0, TT_SAMPLE - TS), (0, 0)))
    p_s, g_s = _sample_proj(xs_pad.reshape(BS * TT_SAMPLE, D), nm, wmain, bmain, wg, bg)
    mix_s, conv_s, C_s, n_s, m_s, S_s = _sample_mixer(
        p_s.reshape(BS, TT_SAMPLE, P_MAIN), g_s.reshape(BS, TT_SAMPLE, GATE_COLS), state_conv[l],
        state_mlstm_C[l], state_mlstm_n[l], state_mlstm_m[l][:, :, None], state_hgrn_S[l],
        convw, convb, mnorm, lbl, hnorm, TS - 1)
    y_sample = _ffn(x_sample.reshape(BS * TS, D), mix_s[:, :TS, :].reshape(BS * TS, D_MODEL), wout, nf, wgate,
                    wup, wdown, nfin).reshape(BS, TS, D)

    return (y_prompt, y_sample,
            conv_p[None], C_p[None], n_p[None], m_p[None, :, :N_HEADS, 0], S_p[None],
            conv_s[None], C_s[None], n_s[None], m_s[None, :, :N_HEADS, 0], S_s[None])
```

```python
import functools

import jax
import jax.numpy as jnp
from jax import lax
from jax.experimental import pallas as pl
from jax.experimental.pallas import tpu as pltpu

D_MODEL = 1024
D_A = 512
D_B = 512
N_HEADS = 4
DH = 128
CONV_W = 4
D_FF = 2816
EPS = 1e-6
P_MAIN = 2 * D_A + D_A + D_A + 4 * D_B
GATE_COLS = 128
SUBLANES = 8
CONV_PAD = 8
VMEM_LIMIT = 56 * 1024 * 1024

TT_PROMPT = 256
CHUNK_PROMPT = 256
TT_SAMPLE = 8
TM_FFN = 512
SEQS_PER_STEP = 16

_F32 = jnp.float32
_BF16 = jnp.bfloat16


def _dot_dims(a, b, ca, cb):
    batch = ((0,), (0,)) if a.ndim == 3 and b.ndim == 3 else ((), ())
    return lax.dot_general(a, b, (((a.ndim + ca,), (b.ndim + cb,)), batch), preferred_element_type=_F32)


def _dot(a, b):
    return _dot_dims(a, b, -1, -2)


def _dot_nt(a, b):
    return _dot_dims(a, b, -1, -1)


def _dot_tn(a, b):
    return _dot_dims(a, b, -2, -2)


def _sigmoid(x):
    return 1.0 / (1.0 + jnp.exp(-x))


def _rmsnorm(x, g):
    return x * lax.rsqrt(jnp.mean(x * x, axis=-1, keepdims=True) + EPS) * g


def _bf16_pieces(x):
    hi = x.astype(_BF16)
    r1 = x - hi.astype(_F32)
    mid = r1.astype(_BF16)
    lo = (r1 - mid.astype(_F32)).astype(_BF16)
    return hi, mid, lo


def _tril_matmul_cumsum(x, tril_bf16):
    hi, mid, lo = _bf16_pieces(x)
    return _dot(tril_bf16, hi) + _dot(tril_bf16, mid) + _dot(tril_bf16, lo)


def _triu_matmul_cumsum(x, triu_bf16):
    hi, mid, lo = _bf16_pieces(x)
    return _dot(hi, triu_bf16) + _dot(mid, triu_bf16) + _dot(lo, triu_bf16)


def _log_sigmoid(x):
    return jnp.minimum(x, 0.0) - jnp.log(1.0 + jnp.exp(-jnp.abs(x)))


def _sublane_cumsum(x, row):
    for sh in (1, 2, 4):
        x = x + jnp.where(row >= sh, pltpu.roll(x, sh, x.ndim - 2), 0.0)
    return x


def _row_at(x, row, t):
    return jnp.sum(jnp.where(row == t, x, 0.0), axis=-2, keepdims=True)


def _gate_vectors_small(igc, lfc, tril, eye):
    lfr = jnp.sum(jnp.where(eye, lfc, 0.0), axis=-2, keepdims=True)
    Fc = jnp.sum(jnp.where(tril, lfr, 0.0), axis=-1, keepdims=True)
    uc = igc - Fc
    ur = jnp.sum(jnp.where(eye, uc, 0.0), axis=-2, keepdims=True)
    return Fc, uc, ur


def _mlstm_head(q, k, v, Fc, uc, ur, C0, n0, m0, tril, row, t_last):
    a = Fc + m0
    D = jnp.where(tril, Fc + ur, -jnp.inf)
    mt = jnp.maximum(a, jnp.max(D, axis=-1, keepdims=True))
    w_inter = jnp.exp(a - mt)
    W = jnp.exp(D - mt)
    qb = q.astype(_BF16)
    kb = k.astype(_BF16)
    vb = v.astype(_BF16)
    s = _dot_nt(qb, kb) * W
    num = w_inter * _dot(qb, C0.astype(_BF16)) + _dot(s.astype(_BF16), vb)
    den = w_inter * jnp.sum(q * n0, axis=-1, keepdims=True) + jnp.sum(s, axis=-1, keepdims=True)
    h = num / jnp.maximum(jnp.abs(den), jnp.exp(-mt))
    F_last = _row_at(Fc, row, t_last)
    m_last = _row_at(mt, row, t_last)
    decay = _row_at(w_inter, row, t_last)
    valid = row <= t_last
    wl = jnp.where(valid, jnp.exp(jnp.where(valid, F_last + uc - m_last, 0.0)), 0.0)
    kw = wl * k
    C_new = decay * C0 + _dot_tn(kw.astype(_BF16), vb)
    n_new = decay * n0 + jnp.sum(kw, axis=-2, keepdims=True)
    return h, C_new, n_new, m_last


def _hgrn_core(q, kk, v, f, b, S0, eye_dh, xor_rc, row, t_last, b_ref, qt_ref, kt_ref, nd_ref):
    n_batch, tt, _ = q.shape
    vb = v.astype(_BF16)
    o = _dot((q * jnp.exp(b)).astype(_BF16), S0.astype(_BF16))

    if nd_ref is None:
        fz = jnp.where((row & (SUBLANES - 1)) == 0, 0.0, f)
        o = o + jnp.sum(q * kk, axis=-1, keepdims=True) * v
        E = fz
        for d in range(1, SUBLANES):
            if d > 1:
                E = E * pltpu.roll(fz, d - 1, 1)
            A = jnp.sum(q * pltpu.roll(kk, d, 1) * E, axis=-1, keepdims=True)
            o = o + A * pltpu.roll(v, d, 1)
    else:
        for idx, val in enumerate((q, kk, v, f)):
            nd_ref[idx] = val
        group = SUBLANES * SUBLANES
        for n in range(n_batch):
            for g0 in range(0, tt, group):
                views = [[nd_ref[idx, n, pl.ds(g0 + pos, SUBLANES, stride=SUBLANES), :]
                          for pos in range(SUBLANES)] for idx in range(4)]
                qv, kv, vv, fv = views
                for tp in range(SUBLANES):
                    acc = jnp.sum(qv[tp] * kv[tp], axis=-1, keepdims=True) * vv[tp]
                    E = None
                    for sp in range(tp - 1, -1, -1):
                        E = fv[sp + 1] if E is None else E * fv[sp + 1]
                        A = jnp.sum(qv[tp] * kv[sp] * E, axis=-1, keepdims=True)
                        acc = acc + A * vv[sp]
                    nd_ref[4, n, pl.ds(g0 + tp, SUBLANES, stride=SUBLANES), :] = acc
        o = o + nd_ref[4]

    if tt > SUBLANES:
        b_ref[...] = b
        A_far = None
        lvl = 0
        h = SUBLANES
        while h < tt:
            cols = slice(lvl * DH, (lvl + 1) * DH)
            for j in range(tt // h):
                rows = slice(j * h, (j + 1) * h)
                if j % 2 == 1:
                    b_mid = b_ref[:, pl.ds(j * h - 1, 1), :]
                    qt_ref[:, rows, cols] = q[:, rows, :] * jnp.exp(b[:, rows, :] - b_mid)
                    kt_ref[:, rows, cols] = jnp.zeros((n_batch, h, DH), _F32)
                else:
                    b_mid = b_ref[:, pl.ds((j + 1) * h - 1, 1), :]
                    kt_ref[:, rows, cols] = kk[:, rows, :] * jnp.exp(b_mid - b[:, rows, :])
                    qt_ref[:, rows, cols] = jnp.zeros((n_batch, h, DH), _F32)
            A_h = _dot_nt(qt_ref[:, :, cols].astype(_BF16), kt_ref[:, :, cols].astype(_BF16))
            if 2 * h < tt:
                A_h = jnp.where(xor_rc < 2 * h, A_h, 0.0)
            A_far = A_h if A_far is None else A_far + A_h
            lvl += 1
            h *= 2
        o = o + _dot(A_far.astype(_BF16), vb)

    bL = _row_at(b, row, t_last)
    valid = row <= t_last
    ks = jnp.where(valid, kk * jnp.exp(jnp.where(valid, bL - b, 0.0)), 0.0)
    bL_col = jnp.sum(jnp.where(eye_dh, bL, 0.0), axis=-1, keepdims=True)
    S_new = jnp.exp(bL_col) * S0 + _dot_tn(ks.astype(_BF16), vb)
    return o, S_new


def _headnorm(h, g):
    return h * lax.rsqrt(jnp.mean(h * h, axis=-1, keepdims=True) + EPS) * g


def _mixers(p_ref, g, convw_ref, convb_ref, mnorm_ref, lbl_ref, hnorm_ref,
            mix_ref, convout_ref, C_in, n_in, m_in, S_in, C_out, n_out, m_out, S_out,
            convbuf, b_ref, qt_ref, kt_ref, nd_ref, tt, t_last):
    batched = len(p_ref.shape) == 3
    n_tiles = p_ref.shape[0] if batched else 1
    lead = (slice(None),) if batched else ()

    def rows(start, size):
        return lead + (pl.ds(start, size), slice(None))

    def cols(start, size):
        return lead + (slice(None), slice(start, start + size))

    def head(hd):
        return lead + (hd,)

    def head_row(hd):
        return lead + (slice(hd, hd + 1), slice(None))

    def cat_heads(get):
        parts = [get(hd) for hd in range(N_HEADS)]
        return jnp.concatenate([x if batched else x[None] for x in parts], axis=0)

    def head_cols(x):
        return cat_heads(lambda hd: x[..., hd * DH:(hd + 1) * DH])

    def head_gain(ref):
        return cat_heads(lambda hd: jnp.broadcast_to(ref[:, hd * DH:(hd + 1) * DH], (n_tiles, 1, DH))
                         if batched else ref[:, hd * DH:(hd + 1) * DH])

    def of_head(x, hd):
        return x[hd * n_tiles:(hd + 1) * n_tiles] if batched else x[hd]

    row = lax.broadcasted_iota(jnp.int32, (tt, 1), 0)
    ri = lax.broadcasted_iota(jnp.int32, (tt, tt), 0)
    ci = lax.broadcasted_iota(jnp.int32, (tt, tt), 1)
    tril = ci <= ri
    eye = ci == ri
    xor_rc = ri ^ ci
    rd = lax.broadcasted_iota(jnp.int32, (DH, DH), 0)
    cd = lax.broadcasted_iota(jnp.int32, (DH, DH), 1)
    eye_dh = rd == cd

    prev = CONV_PAD - (CONV_W - 1)
    last = CONV_PAD + t_last - (CONV_W - 2)
    convbuf[rows(CONV_PAD, tt)] = p_ref[cols(0, 2 * D_A)]
    conv = convb_ref[...]
    for j in range(CONV_W):
        conv = conv + convw_ref[j:j + 1, :] * convbuf[rows(prev + j, tt)]
    new_rows = convbuf[rows(last, CONV_W - 1)]
    convout_ref[...] = new_rows
    convbuf[rows(prev, CONV_W - 1)] = new_rows
    qk = conv * _sigmoid(conv)
    q = head_cols(qk[..., 0:D_A]) * (DH ** -0.5)
    k = head_cols(qk[..., D_A:2 * D_A])

    base = 4 * D_A
    f_raw = p_ref[cols(base + D_B, D_B)]
    logits = lbl_ref[...]
    ex = jnp.exp(logits - jnp.max(logits, axis=0, keepdims=True))
    lb = ex[0:1, :] / jnp.sum(ex, axis=0, keepdims=True)
    f = lb + (1.0 - lb) * _sigmoid(f_raw)
    lf = jnp.log(f)

    lf_all = _log_sigmoid(g)
    if batched or tt == SUBLANES:
        b = _sublane_cumsum(lf, row)
        Fc, uc, ur = _gate_vectors_small(cat_heads(lambda hd: g[..., hd:hd + 1]),
                                         cat_heads(lambda hd: lf_all[..., N_HEADS + hd:N_HEADS + hd + 1]),
                                         tril, eye)
    else:
        tril_bf16 = jnp.where(tril, 1.0, 0.0).astype(_BF16)
        triu_bf16 = jnp.where(ci >= ri, 1.0, 0.0).astype(_BF16)
        cum = _tril_matmul_cumsum(jnp.concatenate([lf, lf_all], axis=1), tril_bf16)
        b = cum[:, 0:D_B]
        g_t = g.T
        F_rows = _triu_matmul_cumsum(_log_sigmoid(g_t[0:2 * N_HEADS, :]), triu_bf16)
        Fc = cat_heads(lambda hd: cum[:, D_B + N_HEADS + hd:D_B + N_HEADS + hd + 1])
        uc = cat_heads(lambda hd: g[:, hd:hd + 1]) - Fc
        ur = cat_heads(lambda hd: g_t[hd:hd + 1, :] - F_rows[N_HEADS + hd:N_HEADS + hd + 1, :])

    v = head_cols(p_ref[cols(2 * D_A, D_A)])
    o_raw = head_cols(p_ref[cols(3 * D_A, D_A)])
    C0 = cat_heads(lambda hd: C_in[head(hd)])
    n0 = cat_heads(lambda hd: n_in[head_row(hd)])
    m0 = cat_heads(m_in)
    h, C_new, n_new, m_new = _mlstm_head(q, k, v, Fc, uc, ur, C0, n0, m0, tril, row, t_last)
    out_a = _sigmoid(o_raw) * _headnorm(h, head_gain(mnorm_ref))
    m_new = jnp.broadcast_to(m_new, m_new.shape[:-1] + (GATE_COLS,))
    for hd in range(N_HEADS):
        C_out[head(hd)] = of_head(C_new, hd)
        n_out[head_row(hd)] = of_head(n_new, hd)
        m_out[head_row(hd)] = of_head(m_new, hd)
        mix_ref[cols(hd * DH, DH)] = of_head(out_a, hd).astype(mix_ref.dtype)

    q_raw = p_ref[cols(base, D_B)]
    g_raw = p_ref[cols(base + 3 * D_B, D_B)]
    kk = (1.0 - lb) * _sigmoid(-f_raw)
    qs = q_raw * _sigmoid(q_raw)
    S0 = cat_heads(lambda hd: S_in[head(hd)])
    o, S_new = _hgrn_core(head_cols(qs), head_cols(kk), head_cols(p_ref[cols(base + 2 * D_B, D_B)]),
                          head_cols(f), head_cols(b), S0, eye_dh, xor_rc, row, t_last,
                          b_ref, qt_ref, kt_ref, nd_ref)
    out_b = _headnorm(o, head_gain(hnorm_ref)) * head_cols(g_raw * _sigmoid(g_raw))
    for hd in range(N_HEADS):
        S_out[head(hd)] = of_head(S_new, hd)
        mix_ref[cols(D_A + hd * DH, DH)] = of_head(out_b, hd).astype(mix_ref.dtype)


def _project(x, nm_ref, wa_ref, wb_ref, bmain_ref, wg_ref, bg_ref):
    h = _rmsnorm(x, nm_ref[...]).astype(_BF16)
    half = P_MAIN // 2
    p = jnp.concatenate([_dot_nt(h, wa_ref[...]) + bmain_ref[:, 0:half],
                         _dot_nt(h, wb_ref[...]) + bmain_ref[:, half:P_MAIN]], axis=1)
    g = _dot_nt(h, wg_ref[...]) + bg_ref[...]
    return p, g


def _prompt_kernel(x0_ref, xa_ref, xb_ref, nm_ref, wa_ref, wb_ref, bmain_ref, wg_ref, bg_ref, convw_ref, convb_ref,
                   mnorm_ref, lbl_ref, hnorm_ref,
                   mix_ref, convp_ref, C_ref, n_ref, m_ref, S_ref,
                   p_a, g_a, p_b, g_b, convbuf, b_ref, qt_ref, kt_ref, nd_ref, steps_per_row):
    tt = xa_ref.shape[0]
    j = pl.program_id(0)

    def project_into(x_ref, p_scr, g_scr):
        p, g = _project(x_ref[...], nm_ref, wa_ref, wb_ref, bmain_ref, wg_ref, bg_ref)
        p_scr[...] = p
        g_scr[...] = g

    def mix(p_scr, g_scr, half):
        ck = CHUNK_PROMPT
        for r0 in range(0, tt, ck):
            _mixers(p_scr.at[pl.ds(r0, ck)], g_scr[pl.ds(r0, ck), :], convw_ref, convb_ref, mnorm_ref, lbl_ref,
                    hnorm_ref, mix_ref.at[pl.ds(half * tt + r0, ck)], convp_ref, C_ref, n_ref,
                    lambda hd: m_ref[hd:hd + 1, 0:1], S_ref, C_ref, n_ref, m_ref, S_ref,
                    convbuf, b_ref, qt_ref, kt_ref, nd_ref, ck, ck - 1)

    @pl.when(j == 0)
    def _():
        project_into(x0_ref, p_a, g_a)

    @pl.when(j % steps_per_row == 0)
    def _():
        C_ref[...] = jnp.zeros_like(C_ref)
        n_ref[...] = jnp.zeros_like(n_ref)
        m_ref[...] = jnp.zeros_like(m_ref)
        S_ref[...] = jnp.zeros_like(S_ref)
        convbuf[0:CONV_PAD, :] = jnp.zeros((CONV_PAD, 2 * D_A), _F32)

    project_into(xa_ref, p_b, g_b)
    mix(p_a, g_a, 0)
    project_into(xb_ref, p_a, g_a)
    mix(p_b, g_b, 1)


def _proj_kernel(x_ref, nm_ref, wa_ref, wb_ref, bmain_ref, wg_ref, bg_ref, p_ref, g_ref):
    p, g = _project(x_ref[...], nm_ref, wa_ref, wb_ref, bmain_ref, wg_ref, bg_ref)
    p_ref[...] = p
    g_ref[...] = g


def _sample_kernel(p_ref, g_ref, conv0_ref, C0_ref, n0_ref, m0_ref, S0_ref,
                   convw_ref, convb_ref, mnorm_ref, lbl_ref, hnorm_ref,
                   mix_ref, convs_ref, C_ref, n_ref, m_ref, S_ref,
                   convbuf, t_last):
    n_seq, tt, _ = p_ref.shape
    lead_rows = CONV_PAD - (CONV_W - 1)
    convbuf[:, 0:lead_rows, :] = jnp.zeros((n_seq, lead_rows, 2 * D_A), _F32)
    convbuf[:, lead_rows:CONV_PAD, :] = conv0_ref[...]
    m_ref[...] = jnp.zeros_like(m_ref)
    _mixers(p_ref, g_ref[...], convw_ref, convb_ref, mnorm_ref, lbl_ref, hnorm_ref,
            mix_ref, convs_ref, C0_ref, n0_ref, lambda hd: m0_ref[:, hd:hd + 1, :], S0_ref,
            C_ref, n_ref, m_ref, S_ref, convbuf, None, None, None, None, tt, t_last)


def _ffn_kernel(x_ref, mix_ref, wout_ref, nf_ref, wgate_ref, wup_ref, wdown_ref, nfin_ref, y_ref):
    x1 = x_ref[...] + _dot(mix_ref[...], wout_ref[...])
    h2 = _rmsnorm(x1, nf_ref[...]).astype(_BF16)
    gate = _dot(h2, wgate_ref[...])
    up = _dot(h2, wup_ref[...])
    ff = (gate * _sigmoid(gate) * up).astype(_BF16)
    x2 = x1 + _dot(ff, wdown_ref[...])
    y_ref[...] = _rmsnorm(x2, nfin_ref[...])


def _resident(shape):
    nd = len(shape)
    return pl.BlockSpec(shape, lambda *_: (0,) * nd, pipeline_mode=pl.Buffered(1))


def _prompt_mixer(x, nm, wa, wb, bmain, wg, bg, convw, convb, mnorm, lbl, hnorm):
    B, T, D = x.shape
    tt = TT_PROMPT
    ck = CHUNK_PROMPT
    nlev = (ck // SUBLANES).bit_length() - 1
    n_tiles = B * T // tt
    steps_per_row = T // (2 * tt)
    x2 = x.reshape(B * T, D)
    out_shape = (
        jax.ShapeDtypeStruct((B * T, D_MODEL), _BF16),
        jax.ShapeDtypeStruct((B, CONV_W - 1, 2 * D_A), _F32),
        jax.ShapeDtypeStruct((B, N_HEADS, DH, DH), _F32),
        jax.ShapeDtypeStruct((B, N_HEADS, DH), _F32),
        jax.ShapeDtypeStruct((B, SUBLANES, GATE_COLS), _F32),
        jax.ShapeDtypeStruct((B, N_HEADS, DH, DH), _F32),
    )
    weights = (nm, wa, wb, bmain, wg, bg, convw, convb, mnorm, lbl, hnorm)
    in_specs = [
        pl.BlockSpec((tt, D), lambda j: (0, 0)),
        pl.BlockSpec((tt, D), lambda j: (2 * j + 1, 0)),
        pl.BlockSpec((tt, D), lambda j: (jnp.minimum(2 * j + 2, n_tiles - 1), 0)),
    ] + [_resident(w.shape) for w in weights]
    out_specs = (
        pl.BlockSpec((2 * tt, D_MODEL), lambda j: (j, 0)),
        pl.BlockSpec((None, CONV_W - 1, 2 * D_A), lambda j: (j // steps_per_row, 0, 0)),
        pl.BlockSpec((None, N_HEADS, DH, DH), lambda j: (j // steps_per_row, 0, 0, 0)),
        pl.BlockSpec((None, N_HEADS, DH), lambda j: (j // steps_per_row, 0, 0)),
        pl.BlockSpec((None, SUBLANES, GATE_COLS), lambda j: (j // steps_per_row, 0, 0)),
        pl.BlockSpec((None, N_HEADS, DH, DH), lambda j: (j // steps_per_row, 0, 0, 0)),
    )
    scratch = [
        pltpu.VMEM((tt, P_MAIN), _F32), pltpu.VMEM((tt, GATE_COLS), _F32),
        pltpu.VMEM((tt, P_MAIN), _F32), pltpu.VMEM((tt, GATE_COLS), _F32),
        pltpu.VMEM((CONV_PAD + ck, 2 * D_A), _F32),
        pltpu.VMEM((N_HEADS, ck, DH), _F32),
        pltpu.VMEM((N_HEADS, ck, nlev * DH), _F32),
        pltpu.VMEM((N_HEADS, ck, nlev * DH), _F32),
        pltpu.VMEM((5, N_HEADS, ck, DH), _F32),
    ]
    return pl.pallas_call(
        functools.partial(_prompt_kernel, steps_per_row=steps_per_row),
        out_shape=out_shape, grid=(n_tiles // 2,), in_specs=in_specs, out_specs=out_specs,
        scratch_shapes=scratch, name="prompt_mixer",
        compiler_params=pltpu.CompilerParams(dimension_semantics=("arbitrary",), vmem_limit_bytes=VMEM_LIMIT),
    )(x2, x2, x2, *weights)


def _sample_proj(x, nm, wa, wb, bmain, wg, bg):
    N, D = x.shape
    tm = min(N, 512)
    return pl.pallas_call(
        _proj_kernel,
        out_shape=(jax.ShapeDtypeStruct((N, P_MAIN), _F32), jax.ShapeDtypeStruct((N, GATE_COLS), _F32)),
        grid=(N // tm,),
        in_specs=[pl.BlockSpec((tm, D), lambda i: (i, 0))] + [_resident(a.shape) for a in (nm, wa, wb, bmain, wg, bg)],
        out_specs=(pl.BlockSpec((tm, P_MAIN), lambda i: (i, 0)), pl.BlockSpec((tm, GATE_COLS), lambda i: (i, 0))),
        name="sample_proj",
        compiler_params=pltpu.CompilerParams(dimension_semantics=("arbitrary",), vmem_limit_bytes=VMEM_LIMIT),
    )(x, nm, wa, wb, bmain, wg, bg)


def _sample_mixer(p, g, conv0, C0, n0, m0, S0, convw, convb, mnorm, lbl, hnorm, t_last):
    B, tt, _ = p.shape
    out_shape = (
        jax.ShapeDtypeStruct((B, tt, D_MODEL), _BF16),
        jax.ShapeDtypeStruct((B, CONV_W - 1, 2 * D_A), _F32),
        jax.ShapeDtypeStruct((B, N_HEADS, DH, DH), _F32),
        jax.ShapeDtypeStruct((B, N_HEADS, DH), _F32),
        jax.ShapeDtypeStruct((B, SUBLANES, GATE_COLS), _F32),
        jax.ShapeDtypeStruct((B, N_HEADS, DH, DH), _F32),
    )
    ns = SEQS_PER_STEP
    in_specs = [
        pl.BlockSpec((ns, tt, P_MAIN), lambda b: (b, 0, 0)),
        pl.BlockSpec((ns, tt, GATE_COLS), lambda b: (b, 0, 0)),
        pl.BlockSpec((ns, CONV_W - 1, 2 * D_A), lambda b: (b, 0, 0)),
        pl.BlockSpec((ns, N_HEADS, DH, DH), lambda b: (b, 0, 0, 0)),
        pl.BlockSpec((ns, N_HEADS, DH), lambda b: (b, 0, 0)),
        pl.BlockSpec((ns, N_HEADS, 1), lambda b: (b, 0, 0)),
        pl.BlockSpec((ns, N_HEADS, DH, DH), lambda b: (b, 0, 0, 0)),
        _resident(convw.shape), _resident(convb.shape), _resident(mnorm.shape), _resident(lbl.shape),
        _resident(hnorm.shape),
    ]
    out_specs = (
        pl.BlockSpec((ns, tt, D_MODEL), lambda b: (b, 0, 0)),
        pl.BlockSpec((ns, CONV_W - 1, 2 * D_A), lambda b: (b, 0, 0)),
        pl.BlockSpec((ns, N_HEADS, DH, DH), lambda b: (b, 0, 0, 0)),
        pl.BlockSpec((ns, N_HEADS, DH), lambda b: (b, 0, 0)),
        pl.BlockSpec((ns, SUBLANES, GATE_COLS), lambda b: (b, 0, 0)),
        pl.BlockSpec((ns, N_HEADS, DH, DH), lambda b: (b, 0, 0, 0)),
    )
    return pl.pallas_call(
        functools.partial(_sample_kernel, t_last=t_last),
        out_shape=out_shape, grid=(B // ns,), in_specs=in_specs, out_specs=out_specs,
        scratch_shapes=[pltpu.VMEM((ns, CONV_PAD + tt, 2 * D_A), _F32)], name="sample_mixer",
        compiler_params=pltpu.CompilerParams(dimension_semantics=("arbitrary",), vmem_limit_bytes=VMEM_LIMIT),
    )(p, g, conv0, C0, n0, m0, S0, convw, convb, mnorm, lbl, hnorm)


def _ffn(x, mix, wout, nf, wgate, wup, wdown, nfin):
    N, D = x.shape
    tm = min(N, TM_FFN)
    return pl.pallas_call(
        _ffn_kernel, out_shape=jax.ShapeDtypeStruct((N, D), _F32), grid=(N // tm,),
        in_specs=[pl.BlockSpec((tm, D), lambda i: (i, 0)), pl.BlockSpec((tm, D_MODEL), lambda i: (i, 0)),
                  _resident(wout.shape), _resident(nf.shape), _resident(wgate.shape), _resident(wup.shape),
                  _resident(wdown.shape), _resident(nfin.shape)],
        out_specs=pl.BlockSpec((tm, D), lambda i: (i, 0)), name="ffn",
        compiler_params=pltpu.CompilerParams(dimension_semantics=("arbitrary",), vmem_limit_bytes=VMEM_LIMIT),
    )(x, mix, wout, nf, wgate, wup, wdown, nfin)


def kernel(x_prompt, x_sample, state_conv, state_mlstm_C, state_mlstm_n, state_mlstm_m, state_hgrn_S,
           norm_mix, w_in, b_in, conv_w, conv_b, mlstm_norm, hgrn_lb_logits, hgrn_norm, w_out,
           norm_ffn, w_gate, w_up, w_down, norm_final):
    depth = w_in.shape[0]
    assert depth == 1, "single-layer stack"
    B, T, D = x_prompt.shape
    BS, TS, _ = x_sample.shape
    l = 0
    g0 = 4 * D_A
    n_gates = 2 * N_HEADS

    wt = jnp.swapaxes(w_in[l], 0, 1)
    wa = wt[:g0].astype(_BF16)
    wb = wt[g0 + n_gates:].astype(_BF16)
    wg = jnp.pad(wt[g0:g0 + n_gates], ((0, GATE_COLS - n_gates), (0, 0))).astype(_BF16)
    bmain = jnp.concatenate([b_in[l, :g0], b_in[l, g0 + n_gates:]])[None, :]
    bg = jnp.pad(b_in[l, g0:g0 + n_gates], (0, GATE_COLS - n_gates))[None, :]
    nm = norm_mix[l][None, :]
    convw = conv_w[l]
    convb = conv_b[l][None, :]
    mnorm = mlstm_norm[l][None, :]
    hnorm = hgrn_norm[l][None, :]
    lbl = hgrn_lb_logits.astype(_F32)
    wout = w_out[l].astype(_BF16)
    wgate = w_gate[l].astype(_BF16)
    wup = w_up[l].astype(_BF16)
    wdown = w_down[l].astype(_BF16)
    nf = norm_ffn[l][None, :]
    nfin = norm_final[None, :]

    mix_p, conv_p, C_p, n_p, m_p, S_p = _prompt_mixer(x_prompt, nm, wa, wb, bmain, wg, bg, convw, convb,
                                                      mnorm, lbl, hnorm)
    y_prompt = _ffn(x_prompt.reshape(B * T, D), mix_p, wout, nf, wgate, wup, wdown, nfin).reshape(B, T, D)

    xs_pad = jnp.pad(x_sample, ((0, 0), (0, TT_SAMPLE - TS), (0, 0)))
    p_s, g_s = _sample_proj(xs_pad.reshape(BS * TT_SAMPLE, D), nm, wa, wb, bmain, wg, bg)
    mix_s, conv_s, C_s, n_s, m_s, S_s = _sample_mixer(
        p_s.reshape(BS, TT_SAMPLE, P_MAIN), g_s.reshape(BS, TT_SAMPLE, GATE_COLS), state_conv[l],
        state_mlstm_C[l], state_mlstm_n[l], state_mlstm_m[l][:, :, None], state_hgrn_S[l],
        convw, convb, mnorm, lbl, hnorm, TS - 1)
    y_sample = _ffn(x_sample.reshape(BS * TS, D), mix_s[:, :TS, :].reshape(BS * TS, D_MODEL), wout, nf, wgate,
                    wup, wdown, nfin).reshape(BS, TS, D)

    return (y_prompt, y_sample,
            conv_p[None], C_p[None], n_p[None], m_p[None, :, :N_HEADS, 0], S_p[None],
            conv_s[None], C_s[None], n_s[None], m_s[None, :, :N_HEADS, 0], S_s[None])
```

```python
import functools

import jax
import jax.numpy as jnp
from jax import lax
from jax.experimental import pallas as pl
from jax.experimental.pallas import tpu as pltpu

D_MODEL = 1024
D_A = 512
D_B = 512
N_HEADS = 4
DH = 128
CONV_W = 4
D_FF = 2816
EPS = 1e-6
P_MAIN = 2 * D_A + D_A + D_A + 4 * D_B
GATE_COLS = 128
SUBLANES = 8
CONV_PAD = 8
VMEM_LIMIT = 56 * 1024 * 1024

TT_PROMPT = 256
CHUNK_PROMPT = 256
TT_SAMPLE = 8
TM_FFN = 512
SEQS_PER_STEP = 16

_F32 = jnp.float32
_BF16 = jnp.bfloat16


def _dot_dims(a, b, ca, cb):
    batch = ((0,), (0,)) if a.ndim == 3 and b.ndim == 3 else ((), ())
    return lax.dot_general(a, b, (((a.ndim + ca,), (b.ndim + cb,)), batch), preferred_element_type=_F32)


def _dot(a, b):
    return _dot_dims(a, b, -1, -2)


def _dot_nt(a, b):
    return _dot_dims(a, b, -1, -1)


def _dot_tn(a, b):
    return _dot_dims(a, b, -2, -2)


def _sigmoid(x):
    return 1.0 / (1.0 + jnp.exp(-x))


def _rmsnorm(x, g):
    return x * lax.rsqrt(jnp.mean(x * x, axis=-1, keepdims=True) + EPS) * g


def _bf16_pieces(x):
    hi = x.astype(_BF16)
    r1 = x - hi.astype(_F32)
    mid = r1.astype(_BF16)
    lo = (r1 - mid.astype(_F32)).astype(_BF16)
    return hi, mid, lo


def _tril_matmul_cumsum(x, tril_bf16):
    hi, mid, lo = _bf16_pieces(x)
    return _dot(tril_bf16, hi) + _dot(tril_bf16, mid) + _dot(tril_bf16, lo)


def _triu_matmul_cumsum(x, triu_bf16):
    hi, mid, lo = _bf16_pieces(x)
    return _dot(hi, triu_bf16) + _dot(mid, triu_bf16) + _dot(lo, triu_bf16)


def _log_sigmoid(x):
    return jnp.minimum(x, 0.0) - jnp.log(1.0 + jnp.exp(-jnp.abs(x)))


def _sublane_cumsum(x, row):
    for sh in (1, 2, 4):
        x = x + jnp.where(row >= sh, pltpu.roll(x, sh, x.ndim - 2), 0.0)
    return x


def _row_at(x, row, t):
    return jnp.sum(jnp.where(row == t, x, 0.0), axis=-2, keepdims=True)


def _gate_vectors_small(igc, lfc, tril, eye):
    lfr = jnp.sum(jnp.where(eye, lfc, 0.0), axis=-2, keepdims=True)
    Fc = jnp.sum(jnp.where(tril, lfr, 0.0), axis=-1, keepdims=True)
    uc = igc - Fc
    ur = jnp.sum(jnp.where(eye, uc, 0.0), axis=-2, keepdims=True)
    return Fc, uc, ur


def _mlstm_head(q, k, v, Fc, uc, ur, C0, n0, m0, tril, row, t_last, fill):
    a = Fc + m0
    D = jnp.where(tril, Fc + ur, -jnp.inf)
    mt = jnp.maximum(a, jnp.max(D, axis=-1, keepdims=True))
    w_inter = jnp.exp(a - mt)
    W = jnp.exp(D - mt)
    fill()
    qb = q.astype(_BF16)
    kb = k.astype(_BF16)
    vb = v.astype(_BF16)
    s = _dot_nt(qb, kb) * W
    fill()
    num = w_inter * _dot(qb, C0.astype(_BF16)) + _dot(s.astype(_BF16), vb)
    den = w_inter * jnp.sum(q * n0, axis=-1, keepdims=True) + jnp.sum(s, axis=-1, keepdims=True)
    h = num / jnp.maximum(jnp.abs(den), jnp.exp(-mt))
    fill()
    F_last = _row_at(Fc, row, t_last)
    m_last = _row_at(mt, row, t_last)
    decay = _row_at(w_inter, row, t_last)
    valid = row <= t_last
    wl = jnp.where(valid, jnp.exp(jnp.where(valid, F_last + uc - m_last, 0.0)), 0.0)
    kw = wl * k
    C_new = decay * C0 + _dot_tn(kw.astype(_BF16), vb)
    n_new = decay * n0 + jnp.sum(kw, axis=-2, keepdims=True)
    return h, C_new, n_new, m_last


def _hgrn_core(q, kk, v, f, b, S0, eye_dh, xor_rc, row, t_last, b_ref, qt_ref, kt_ref, nd_ref, fill):
    n_batch, tt, _ = q.shape
    vb = v.astype(_BF16)
    o = _dot((q * jnp.exp(b)).astype(_BF16), S0.astype(_BF16))
    fill()

    if nd_ref is None:
        fz = jnp.where((row & (SUBLANES - 1)) == 0, 0.0, f)
        o = o + jnp.sum(q * kk, axis=-1, keepdims=True) * v
        E = fz
        for d in range(1, SUBLANES):
            if d > 1:
                E = E * pltpu.roll(fz, d - 1, 1)
            A = jnp.sum(q * pltpu.roll(kk, d, 1) * E, axis=-1, keepdims=True)
            o = o + A * pltpu.roll(v, d, 1)
    else:
        for idx, val in enumerate((q, kk, v, f)):
            nd_ref[idx] = val
        group = SUBLANES * SUBLANES
        for n in range(n_batch):
            for g0 in range(0, tt, group):
                views = [[nd_ref[idx, n, pl.ds(g0 + pos, SUBLANES, stride=SUBLANES), :]
                          for pos in range(SUBLANES)] for idx in range(4)]
                qv, kv, vv, fv = views
                for tp in range(SUBLANES):
                    acc = jnp.sum(qv[tp] * kv[tp], axis=-1, keepdims=True) * vv[tp]
                    E = None
                    for sp in range(tp - 1, -1, -1):
                        E = fv[sp + 1] if E is None else E * fv[sp + 1]
                        A = jnp.sum(qv[tp] * kv[sp] * E, axis=-1, keepdims=True)
                        acc = acc + A * vv[sp]
                    nd_ref[4, n, pl.ds(g0 + tp, SUBLANES, stride=SUBLANES), :] = acc
            fill()
        o = o + nd_ref[4]

    if tt > SUBLANES:
        b_ref[...] = b
        A_far = None
        lvl = 0
        h = SUBLANES
        while h < tt:
            cols = slice(lvl * DH, (lvl + 1) * DH)
            for j in range(tt // h):
                rows = slice(j * h, (j + 1) * h)
                if j % 2 == 1:
                    b_mid = b_ref[:, pl.ds(j * h - 1, 1), :]
                    qt_ref[:, rows, cols] = q[:, rows, :] * jnp.exp(b[:, rows, :] - b_mid)
                    kt_ref[:, rows, cols] = jnp.zeros((n_batch, h, DH), _F32)
                else:
                    b_mid = b_ref[:, pl.ds((j + 1) * h - 1, 1), :]
                    kt_ref[:, rows, cols] = kk[:, rows, :] * jnp.exp(b_mid - b[:, rows, :])
                    qt_ref[:, rows, cols] = jnp.zeros((n_batch, h, DH), _F32)
            A_h = _dot_nt(qt_ref[:, :, cols].astype(_BF16), kt_ref[:, :, cols].astype(_BF16))
            if 2 * h < tt:
                A_h = jnp.where(xor_rc < 2 * h, A_h, 0.0)
            A_far = A_h if A_far is None else A_far + A_h
            fill()
            lvl += 1
            h *= 2
        o = o + _dot(A_far.astype(_BF16), vb)

    bL = _row_at(b, row, t_last)
    valid = row <= t_last
    ks = jnp.where(valid, kk * jnp.exp(jnp.where(valid, bL - b, 0.0)), 0.0)
    bL_col = jnp.sum(jnp.where(eye_dh, bL, 0.0), axis=-1, keepdims=True)
    S_new = jnp.exp(bL_col) * S0 + _dot_tn(ks.astype(_BF16), vb)
    return o, S_new


def _headnorm(h, g):
    return h * lax.rsqrt(jnp.mean(h * h, axis=-1, keepdims=True) + EPS) * g


def _mixers(p_ref, g, convw_ref, convb_ref, mnorm_ref, lbl_ref, hnorm_ref,
            mix_ref, convout_ref, C_in, n_in, m_in, S_in, C_out, n_out, m_out, S_out,
            convbuf, b_ref, qt_ref, kt_ref, nd_ref, tt, t_last, fill=lambda: None):
    batched = len(p_ref.shape) == 3
    n_tiles = p_ref.shape[0] if batched else 1
    lead = (slice(None),) if batched else ()

    def rows(start, size):
        return lead + (pl.ds(start, size), slice(None))

    def cols(start, size):
        return lead + (slice(None), slice(start, start + size))

    def head(hd):
        return lead + (hd,)

    def head_row(hd):
        return lead + (slice(hd, hd + 1), slice(None))

    def cat_heads(get):
        parts = [get(hd) for hd in range(N_HEADS)]
        return jnp.concatenate([x if batched else x[None] for x in parts], axis=0)

    def head_cols(x):
        return cat_heads(lambda hd: x[..., hd * DH:(hd + 1) * DH])

    def head_gain(ref):
        return cat_heads(lambda hd: jnp.broadcast_to(ref[:, hd * DH:(hd + 1) * DH], (n_tiles, 1, DH))
                         if batched else ref[:, hd * DH:(hd + 1) * DH])

    def of_head(x, hd):
        return x[hd * n_tiles:(hd + 1) * n_tiles] if batched else x[hd]

    row = lax.broadcasted_iota(jnp.int32, (tt, 1), 0)
    ri = lax.broadcasted_iota(jnp.int32, (tt, tt), 0)
    ci = lax.broadcasted_iota(jnp.int32, (tt, tt), 1)
    tril = ci <= ri
    eye = ci == ri
    xor_rc = ri ^ ci
    rd = lax.broadcasted_iota(jnp.int32, (DH, DH), 0)
    cd = lax.broadcasted_iota(jnp.int32, (DH, DH), 1)
    eye_dh = rd == cd

    prev = CONV_PAD - (CONV_W - 1)
    last = CONV_PAD + t_last - (CONV_W - 2)
    convbuf[rows(CONV_PAD, tt)] = p_ref[cols(0, 2 * D_A)]
    conv = convb_ref[...]
    for j in range(CONV_W):
        conv = conv + convw_ref[j:j + 1, :] * convbuf[rows(prev + j, tt)]
    new_rows = convbuf[rows(last, CONV_W - 1)]
    convout_ref[...] = new_rows
    convbuf[rows(prev, CONV_W - 1)] = new_rows
    fill()
    qk = conv * _sigmoid(conv)
    fill()
    q = head_cols(qk[..., 0:D_A]) * (DH ** -0.5)
    k = head_cols(qk[..., D_A:2 * D_A])

    base = 4 * D_A
    f_raw = p_ref[cols(base + D_B, D_B)]
    logits = lbl_ref[...]
    ex = jnp.exp(logits - jnp.max(logits, axis=0, keepdims=True))
    lb = ex[0:1, :] / jnp.sum(ex, axis=0, keepdims=True)
    f = lb + (1.0 - lb) * _sigmoid(f_raw)
    lf = jnp.log(f)

    lf_all = _log_sigmoid(g)
    if batched or tt == SUBLANES:
        b = _sublane_cumsum(lf, row)
        Fc, uc, ur = _gate_vectors_small(cat_heads(lambda hd: g[..., hd:hd + 1]),
                                         cat_heads(lambda hd: lf_all[..., N_HEADS + hd:N_HEADS + hd + 1]),
                                         tril, eye)
    else:
        tril_bf16 = jnp.where(tril, 1.0, 0.0).astype(_BF16)
        triu_bf16 = jnp.where(ci >= ri, 1.0, 0.0).astype(_BF16)
        cum = _tril_matmul_cumsum(jnp.concatenate([lf, lf_all], axis=1), tril_bf16)
        b = cum[:, 0:D_B]
        g_t = g.T
        F_rows = _triu_matmul_cumsum(_log_sigmoid(g_t[0:2 * N_HEADS, :]), triu_bf16)
        Fc = cat_heads(lambda hd: cum[:, D_B + N_HEADS + hd:D_B + N_HEADS + hd + 1])
        uc = cat_heads(lambda hd: g[:, hd:hd + 1]) - Fc
        ur = cat_heads(lambda hd: g_t[hd:hd + 1, :] - F_rows[N_HEADS + hd:N_HEADS + hd + 1, :])

    fill()
    v = head_cols(p_ref[cols(2 * D_A, D_A)])
    o_raw = head_cols(p_ref[cols(3 * D_A, D_A)])
    C0 = cat_heads(lambda hd: C_in[head(hd)])
    n0 = cat_heads(lambda hd: n_in[head_row(hd)])
    m0 = cat_heads(m_in)
    h, C_new, n_new, m_new = _mlstm_head(q, k, v, Fc, uc, ur, C0, n0, m0, tril, row, t_last, fill)
    out_a = _sigmoid(o_raw) * _headnorm(h, head_gain(mnorm_ref))
    m_new = jnp.broadcast_to(m_new, m_new.shape[:-1] + (GATE_COLS,))
    for hd in range(N_HEADS):
        C_out[head(hd)] = of_head(C_new, hd)
        n_out[head_row(hd)] = of_head(n_new, hd)
        m_out[head_row(hd)] = of_head(m_new, hd)
        mix_ref[cols(hd * DH, DH)] = of_head(out_a, hd).astype(mix_ref.dtype)

    q_raw = p_ref[cols(base, D_B)]
    g_raw = p_ref[cols(base + 3 * D_B, D_B)]
    kk = (1.0 - lb) * _sigmoid(-f_raw)
    qs = q_raw * _sigmoid(q_raw)
    S0 = cat_heads(lambda hd: S_in[head(hd)])
    o, S_new = _hgrn_core(head_cols(qs), head_cols(kk), head_cols(p_ref[cols(base + 2 * D_B, D_B)]),
                          head_cols(f), head_cols(b), S0, eye_dh, xor_rc, row, t_last,
                          b_ref, qt_ref, kt_ref, nd_ref, fill)
    out_b = _headnorm(o, head_gain(hnorm_ref)) * head_cols(g_raw * _sigmoid(g_raw))
    for hd in range(N_HEADS):
        S_out[head(hd)] = of_head(S_new, hd)
        mix_ref[cols(D_A + hd * DH, DH)] = of_head(out_b, hd).astype(mix_ref.dtype)


def _project(x, nm_ref, wa_ref, wb_ref, bmain_ref, wg_ref, bg_ref):
    h = _rmsnorm(x, nm_ref[...]).astype(_BF16)
    half = P_MAIN // 2
    p = jnp.concatenate([_dot_nt(h, wa_ref[...]) + bmain_ref[:, 0:half],
                         _dot_nt(h, wb_ref[...]) + bmain_ref[:, half:P_MAIN]], axis=1)
    g = _dot_nt(h, wg_ref[...]) + bg_ref[...]
    return p, g


def _prompt_kernel(x0_ref, xa_ref, xb_ref, nm_ref, wa_ref, wb_ref, bmain_ref, wg_ref, bg_ref, convw_ref, convb_ref,
                   mnorm_ref, lbl_ref, hnorm_ref,
                   mix_ref, convp_ref, C_ref, n_ref, m_ref, S_ref,
                   p_a, g_a, p_b, g_b, convbuf, b_ref, qt_ref, kt_ref, nd_ref, steps_per_row):
    tt = xa_ref.shape[0]
    j = pl.program_id(0)

    n_chunk = 8
    cw = P_MAIN // n_chunk

    def projection_pieces(x_ref, p_scr, g_scr):
        held = {}

        def norm():
            held["h"] = _rmsnorm(x_ref[...], nm_ref[...]).astype(_BF16)

        def chunk(c):
            w_ref = wa_ref if c < n_chunk // 2 else wb_ref
            r0 = (c % (n_chunk // 2)) * cw

            def emit():
                p_scr[:, c * cw:(c + 1) * cw] = (_dot_nt(held["h"], w_ref[pl.ds(r0, cw), :])
                                                 + bmain_ref[:, c * cw:(c + 1) * cw])
            return emit

        def gates():
            g_scr[...] = _dot_nt(held["h"], wg_ref[...]) + bg_ref[...]

        return [norm] + [chunk(c) for c in range(n_chunk)] + [gates]

    def mix(p_scr, g_scr, half, pieces):
        def fill():
            if pieces:
                pieces.pop(0)()

        ck = CHUNK_PROMPT
        for r0 in range(0, tt, ck):
            _mixers(p_scr.at[pl.ds(r0, ck)], g_scr[pl.ds(r0, ck), :], convw_ref, convb_ref, mnorm_ref, lbl_ref,
                    hnorm_ref, mix_ref.at[pl.ds(half * tt + r0, ck)], convp_ref, C_ref, n_ref,
                    lambda hd: m_ref[hd:hd + 1, 0:1], S_ref, C_ref, n_ref, m_ref, S_ref,
                    convbuf, b_ref, qt_ref, kt_ref, nd_ref, ck, ck - 1, fill)
        while pieces:
            fill()

    @pl.when(j == 0)
    def _():
        for piece in projection_pieces(x0_ref, p_a, g_a):
            piece()

    @pl.when(j % steps_per_row == 0)
    def _():
        C_ref[...] = jnp.zeros_like(C_ref)
        n_ref[...] = jnp.zeros_like(n_ref)
        m_ref[...] = jnp.zeros_like(m_ref)
        S_ref[...] = jnp.zeros_like(S_ref)
        convbuf[0:CONV_PAD, :] = jnp.zeros((CONV_PAD, 2 * D_A), _F32)

    mix(p_a, g_a, 0, projection_pieces(xa_ref, p_b, g_b))
    mix(p_b, g_b, 1, projection_pieces(xb_ref, p_a, g_a))


def _proj_kernel(x_ref, nm_ref, wa_ref, wb_ref, bmain_ref, wg_ref, bg_ref, p_ref, g_ref):
    p, g = _project(x_ref[...], nm_ref, wa_ref, wb_ref, bmain_ref, wg_ref, bg_ref)
    p_ref[...] = p
    g_ref[...] = g


def _sample_kernel(p_ref, g_ref, conv0_ref, C0_ref, n0_ref, m0_ref, S0_ref,
                   convw_ref, convb_ref, mnorm_ref, lbl_ref, hnorm_ref,
                   mix_ref, convs_ref, C_ref, n_ref, m_ref, S_ref,
                   convbuf, t_last):
    n_seq, tt, _ = p_ref.shape
    lead_rows = CONV_PAD - (CONV_W - 1)
    convbuf[:, 0:lead_rows, :] = jnp.zeros((n_seq, lead_rows, 2 * D_A), _F32)
    convbuf[:, lead_rows:CONV_PAD, :] = conv0_ref[...]
    m_ref[...] = jnp.zeros_like(m_ref)
    _mixers(p_ref, g_ref[...], convw_ref, convb_ref, mnorm_ref, lbl_ref, hnorm_ref,
            mix_ref, convs_ref, C0_ref, n0_ref, lambda hd: m0_ref[:, hd:hd + 1, :], S0_ref,
            C_ref, n_ref, m_ref, S_ref, convbuf, None, None, None, None, tt, t_last)


def _ffn_kernel(x_ref, mix_ref, wout_ref, nf_ref, wgate_ref, wup_ref, wdown_ref, nfin_ref, y_ref):
    x1 = x_ref[...] + _dot(mix_ref[...], wout_ref[...])
    h2 = _rmsnorm(x1, nf_ref[...]).astype(_BF16)
    gate = _dot(h2, wgate_ref[...])
    up = _dot(h2, wup_ref[...])
    ff = (gate * _sigmoid(gate) * up).astype(_BF16)
    x2 = x1 + _dot(ff, wdown_ref[...])
    y_ref[...] = _rmsnorm(x2, nfin_ref[...])


def _resident(shape):
    nd = len(shape)
    return pl.BlockSpec(shape, lambda *_: (0,) * nd, pipeline_mode=pl.Buffered(1))


def _prompt_mixer(x, nm, wa, wb, bmain, wg, bg, convw, convb, mnorm, lbl, hnorm):
    B, T, D = x.shape
    tt = TT_PROMPT
    ck = CHUNK_PROMPT
    nlev = (ck // SUBLANES).bit_length() - 1
    n_tiles = B * T // tt
    steps_per_row = T // (2 * tt)
    x2 = x.reshape(B * T, D)
    out_shape = (
        jax.ShapeDtypeStruct((B * T, D_MODEL), _BF16),
        jax.ShapeDtypeStruct((B, CONV_W - 1, 2 * D_A), _F32),
        jax.ShapeDtypeStruct((B, N_HEADS, DH, DH), _F32),
        jax.ShapeDtypeStruct((B, N_HEADS, DH), _F32),
        jax.ShapeDtypeStruct((B, SUBLANES, GATE_COLS), _F32),
        jax.ShapeDtypeStruct((B, N_HEADS, DH, DH), _F32),
    )
    weights = (nm, wa, wb, bmain, wg, bg, convw, convb, mnorm, lbl, hnorm)
    in_specs = [
        pl.BlockSpec((tt, D), lambda j: (0, 0)),
        pl.BlockSpec((tt, D), lambda j: (2 * j + 1, 0)),
        pl.BlockSpec((tt, D), lambda j: (jnp.minimum(2 * j + 2, n_tiles - 1), 0)),
    ] + [_resident(w.shape) for w in weights]
    out_specs = (
        pl.BlockSpec((2 * tt, D_MODEL), lambda j: (j, 0)),
        pl.BlockSpec((None, CONV_W - 1, 2 * D_A), lambda j: (j // steps_per_row, 0, 0)),
        pl.BlockSpec((None, N_HEADS, DH, DH), lambda j: (j // steps_per_row, 0, 0, 0)),
        pl.BlockSpec((None, N_HEADS, DH), lambda j: (j // steps_per_row, 0, 0)),
        pl.BlockSpec((None, SUBLANES, GATE_COLS), lambda j: (j // steps_per_row, 0, 0)),
        pl.BlockSpec((None, N_HEADS, DH, DH), lambda j: (j // steps_per_row, 0, 0, 0)),
    )
    scratch = [
        pltpu.VMEM((tt, P_MAIN), _F32), pltpu.VMEM((tt, GATE_COLS), _F32),
        pltpu.VMEM((tt, P_MAIN), _F32), pltpu.VMEM((tt, GATE_COLS), _F32),
        pltpu.VMEM((CONV_PAD + ck, 2 * D_A), _F32),
        pltpu.VMEM((N_HEADS, ck, DH), _F32),
        pltpu.VMEM((N_HEADS, ck, nlev * DH), _F32),
        pltpu.VMEM((N_HEADS, ck, nlev * DH), _F32),
        pltpu.VMEM((5, N_HEADS, ck, DH), _F32),
    ]
    return pl.pallas_call(
        functools.partial(_prompt_kernel, steps_per_row=steps_per_row),
        out_shape=out_shape, grid=(n_tiles // 2,), in_specs=in_specs, out_specs=out_specs,
        scratch_shapes=scratch, name="prompt_mixer",
        compiler_params=pltpu.CompilerParams(dimension_semantics=("arbitrary",), vmem_limit_bytes=VMEM_LIMIT),
    )(x2, x2, x2, *weights)


def _sample_proj(x, nm, wa, wb, bmain, wg, bg):
    N, D = x.shape
    tm = min(N, 512)
    return pl.pallas_call(
        _proj_kernel,
        out_shape=(jax.ShapeDtypeStruct((N, P_MAIN), _F32), jax.ShapeDtypeStruct((N, GATE_COLS), _F32)),
        grid=(N // tm,),
        in_specs=[pl.BlockSpec((tm, D), lambda i: (i, 0))] + [_resident(a.shape) for a in (nm, wa, wb, bmain, wg, bg)],
        out_specs=(pl.BlockSpec((tm, P_MAIN), lambda i: (i, 0)), pl.BlockSpec((tm, GATE_COLS), lambda i: (i, 0))),
        name="sample_proj",
        compiler_params=pltpu.CompilerParams(dimension_semantics=("arbitrary",), vmem_limit_bytes=VMEM_LIMIT),
    )(x, nm, wa, wb, bmain, wg, bg)


def _sample_mixer(p, g, conv0, C0, n0, m0, S0, convw, convb, mnorm, lbl, hnorm, t_last):
    B, tt, _ = p.shape
    out_shape = (
        jax.ShapeDtypeStruct((B, tt, D_MODEL), _BF16),
        jax.ShapeDtypeStruct((B, CONV_W - 1, 2 * D_A), _F32),
        jax.ShapeDtypeStruct((B, N_HEADS, DH, DH), _F32),
        jax.ShapeDtypeStruct((B, N_HEADS, DH), _F32),
        jax.ShapeDtypeStruct((B, SUBLANES, GATE_COLS), _F32),
        jax.ShapeDtypeStruct((B, N_HEADS, DH, DH), _F32),
    )
    ns = SEQS_PER_STEP
    in_specs = [
        pl.BlockSpec((ns, tt, P_MAIN), lambda b: (b, 0, 0)),
        pl.BlockSpec((ns, tt, GATE_COLS), lambda b: (b, 0, 0)),
        pl.BlockSpec((ns, CONV_W - 1, 2 * D_A), lambda b: (b, 0, 0)),
        pl.BlockSpec((ns, N_HEADS, DH, DH), lambda b: (b, 0, 0, 0)),
        pl.BlockSpec((ns, N_HEADS, DH), lambda b: (b, 0, 0)),
        pl.BlockSpec((ns, N_HEADS, 1), lambda b: (b, 0, 0)),
        pl.BlockSpec((ns, N_HEADS, DH, DH), lambda b: (b, 0, 0, 0)),
        _resident(convw.shape), _resident(convb.shape), _resident(mnorm.shape), _resident(lbl.shape),
        _resident(hnorm.shape),
    ]
    out_specs = (
        pl.BlockSpec((ns, tt, D_MODEL), lambda b: (b, 0, 0)),
        pl.BlockSpec((ns, CONV_W - 1, 2 * D_A), lambda b: (b, 0, 0)),
        pl.BlockSpec((ns, N_HEADS, DH, DH), lambda b: (b, 0, 0, 0)),
        pl.BlockSpec((ns, N_HEADS, DH), lambda b: (b, 0, 0)),
        pl.BlockSpec((ns, SUBLANES, GATE_COLS), lambda b: (b, 0, 0)),
        pl.BlockSpec((ns, N_HEADS, DH, DH), lambda b: (b, 0, 0, 0)),
    )
    return pl.pallas_call(
        functools.partial(_sample_kernel, t_last=t_last),
        out_shape=out_shape, grid=(B // ns,), in_specs=in_specs, out_specs=out_specs,
        scratch_shapes=[pltpu.VMEM((ns, CONV_PAD + tt, 2 * D_A), _F32)], name="sample_mixer",
        compiler_params=pltpu.CompilerParams(dimension_semantics=("arbitrary",), vmem_limit_bytes=VMEM_LIMIT),
    )(p, g, conv0, C0, n0, m0, S0, convw, convb, mnorm, lbl, hnorm)


def _ffn(x, mix, wout, nf, wgate, wup, wdown, nfin):
    N, D = x.shape
    tm = min(N, TM_FFN)
    return pl.pallas_call(
        _ffn_kernel, out_shape=jax.ShapeDtypeStruct((N, D), _F32), grid=(N // tm,),
        in_specs=[pl.BlockSpec((tm, D), lambda i: (i, 0)), pl.BlockSpec((tm, D_MODEL), lambda i: (i, 0)),
                  _resident(wout.shape), _resident(nf.shape), _resident(wgate.shape), _resident(wup.shape),
                  _resident(wdown.shape), _resident(nfin.shape)],
        out_specs=pl.BlockSpec((tm, D), lambda i: (i, 0)), name="ffn",
        compiler_params=pltpu.CompilerParams(dimension_semantics=("arbitrary",), vmem_limit_bytes=VMEM_LIMIT),
    )(x, mix, wout, nf, wgate, wup, wdown, nfin)


def kernel(x_prompt, x_sample, state_conv, state_mlstm_C, state_mlstm_n, state_mlstm_m, state_hgrn_S,
           norm_mix, w_in, b_in, conv_w, conv_b, mlstm_norm, hgrn_lb_logits, hgrn_norm, w_out,
           norm_ffn, w_gate, w_up, w_down, norm_final):
    depth = w_in.shape[0]
    assert depth == 1, "single-layer stack"
    B, T, D = x_prompt.shape
    BS, TS, _ = x_sample.shape
    l = 0
    g0 = 4 * D_A
    n_gates = 2 * N_HEADS

    wt = jnp.swapaxes(w_in[l], 0, 1)
    wa = wt[:g0].astype(_BF16)
    wb = wt[g0 + n_gates:].astype(_BF16)
    wg = jnp.pad(wt[g0:g0 + n_gates], ((0, GATE_COLS - n_gates), (0, 0))).astype(_BF16)
    bmain = jnp.concatenate([b_in[l, :g0], b_in[l, g0 + n_gates:]])[None, :]
    bg = jnp.pad(b_in[l, g0:g0 + n_gates], (0, GATE_COLS - n_gates))[None, :]
    nm = norm_mix[l][None, :]
    convw = conv_w[l]
    convb = conv_b[l][None, :]
    mnorm = mlstm_norm[l][None, :]
    hnorm = hgrn_norm[l][None, :]
    lbl = hgrn_lb_logits.astype(_F32)
    wout = w_out[l].astype(_BF16)
    wgate = w_gate[l].astype(_BF16)
    wup = w_up[l].astype(_BF16)
    wdown = w_down[l].astype(_BF16)
    nf = norm_ffn[l][None, :]
    nfin = norm_final[None, :]

    mix_p, conv_p, C_p, n_p, m_p, S_p = _prompt_mixer(x_prompt, nm, wa, wb, bmain, wg, bg, convw, convb,
                                                      mnorm, lbl, hnorm)
    y_prompt = _ffn(x_prompt.reshape(B * T, D), mix_p, wout, nf, wgate, wup, wdown, nfin).reshape(B, T, D)

    xs_pad = jnp.pad(x_sample, ((0, 0), (0, TT_SAMPLE - TS), (0, 0)))
    p_s, g_s = _sample_proj(xs_pad.reshape(BS * TT_SAMPLE, D), nm, wa, wb, bmain, wg, bg)
    mix_s, conv_s, C_s, n_s, m_s, S_s = _sample_mixer(
        p_s.reshape(BS, TT_SAMPLE, P_MAIN), g_s.reshape(BS, TT_SAMPLE, GATE_COLS), state_conv[l],
        state_mlstm_C[l], state_mlstm_n[l], state_mlstm_m[l][:, :, None], state_hgrn_S[l],
        convw, convb, mnorm, lbl, hnorm, TS - 1)
    y_sample = _ffn(x_sample.reshape(BS * TS, D), mix_s[:, :TS, :].reshape(BS * TS, D_MODEL), wout, nf, wgate,
                    wup, wdown, nfin).reshape(BS, TS, D)

    return (y_prompt, y_sample,
            conv_p[None], C_p[None], n_p[None], m_p[None, :, :N_HEADS, 0], S_p[None],
            conv_s[None], C_s[None], n_s[None], m_s[None, :, :N_HEADS, 0], S_s[None])
```

```python
import functools

import jax
import jax.numpy as jnp
from jax import lax
from jax.experimental import pallas as pl
from jax.experimental.pallas import tpu as pltpu

D_MODEL = 1024
D_A = 512
D_B = 512
N_HEADS = 4
DH = 128
CONV_W = 4
D_FF = 2816
EPS = 1e-6
P_MAIN = 2 * D_A + D_A + D_A + 4 * D_B
GATE_COLS = 128
SUBLANES = 8
CONV_PAD = 8
VMEM_LIMIT = 56 * 1024 * 1024

TT_PROMPT = 256
CHUNK_PROMPT = 256
TT_SAMPLE = 8
TM_FFN = 512
SEQS_PER_STEP = 16

_F32 = jnp.float32
_BF16 = jnp.bfloat16


def _dot_dims(a, b, ca, cb):
    batch = ((0,), (0,)) if a.ndim == 3 and b.ndim == 3 else ((), ())
    return lax.dot_general(a, b, (((a.ndim + ca,), (b.ndim + cb,)), batch), preferred_element_type=_F32)


def _dot(a, b):
    return _dot_dims(a, b, -1, -2)


def _dot_nt(a, b):
    return _dot_dims(a, b, -1, -1)


def _dot_tn(a, b):
    return _dot_dims(a, b, -2, -2)


def _sigmoid(x):
    return 1.0 / (1.0 + jnp.exp(-x))


def _rmsnorm(x, g):
    return x * lax.rsqrt(jnp.mean(x * x, axis=-1, keepdims=True) + EPS) * g


def _bf16_pieces(x):
    hi = x.astype(_BF16)
    r1 = x - hi.astype(_F32)
    mid = r1.astype(_BF16)
    lo = (r1 - mid.astype(_F32)).astype(_BF16)
    return hi, mid, lo


def _tril_matmul_cumsum(x, tril_bf16):
    hi, mid, lo = _bf16_pieces(x)
    return _dot(tril_bf16, hi) + _dot(tril_bf16, mid) + _dot(tril_bf16, lo)


def _triu_matmul_cumsum(x, triu_bf16):
    hi, mid, lo = _bf16_pieces(x)
    return _dot(hi, triu_bf16) + _dot(mid, triu_bf16) + _dot(lo, triu_bf16)


def _log_sigmoid(x):
    return jnp.minimum(x, 0.0) - jnp.log(1.0 + jnp.exp(-jnp.abs(x)))


def _sublane_cumsum(x, row):
    for sh in (1, 2, 4):
        x = x + jnp.where(row >= sh, pltpu.roll(x, sh, x.ndim - 2), 0.0)
    return x


def _row_at(x, row, t):
    return jnp.sum(jnp.where(row == t, x, 0.0), axis=-2, keepdims=True)


def _gate_vectors_small(igc, lfc, tril, eye):
    lfr = jnp.sum(jnp.where(eye, lfc, 0.0), axis=-2, keepdims=True)
    Fc = jnp.sum(jnp.where(tril, lfr, 0.0), axis=-1, keepdims=True)
    uc = igc - Fc
    ur = jnp.sum(jnp.where(eye, uc, 0.0), axis=-2, keepdims=True)
    return Fc, uc, ur


def _mlstm_head(q, k, v, Fc, uc, ur, C0, n0, m0, tril, row, t_last, fill):
    a = Fc + m0
    D = jnp.where(tril, Fc + ur, -jnp.inf)
    mt = jnp.maximum(a, jnp.max(D, axis=-1, keepdims=True))
    w_inter = jnp.exp(a - mt)
    W = jnp.exp(D - mt)
    fill()
    qb = q.astype(_BF16)
    kb = k.astype(_BF16)
    vb = v.astype(_BF16)
    s = _dot_nt(qb, kb) * W
    fill()
    num = w_inter * _dot(qb, C0.astype(_BF16)) + _dot(s.astype(_BF16), vb)
    den = w_inter * jnp.sum(q * n0, axis=-1, keepdims=True) + jnp.sum(s, axis=-1, keepdims=True)
    h = num / jnp.maximum(jnp.abs(den), jnp.exp(-mt))
    fill()
    F_last = _row_at(Fc, row, t_last)
    m_last = _row_at(mt, row, t_last)
    decay = _row_at(w_inter, row, t_last)
    valid = row <= t_last
    wl = jnp.where(valid, jnp.exp(jnp.where(valid, F_last + uc - m_last, 0.0)), 0.0)
    kw = wl * k
    C_new = decay * C0 + _dot_tn(kw.astype(_BF16), vb)
    n_new = decay * n0 + jnp.sum(kw, axis=-2, keepdims=True)
    return h, C_new, n_new, m_last


def _hgrn_core(q, kk, v, f, b, S0, eye_dh, xor_rc, row, t_last, b_ref, qt_ref, kt_ref, nd_ref, fill):
    n_batch, tt, _ = q.shape
    vb = v.astype(_BF16)
    o = _dot((q * jnp.exp(b)).astype(_BF16), S0.astype(_BF16))
    fill()

    if nd_ref is None:
        fz = jnp.where((row & (SUBLANES - 1)) == 0, 0.0, f)
        o = o + jnp.sum(q * kk, axis=-1, keepdims=True) * v
        E = fz
        for d in range(1, SUBLANES):
            if d > 1:
                E = E * pltpu.roll(fz, d - 1, 1)
            A = jnp.sum(q * pltpu.roll(kk, d, 1) * E, axis=-1, keepdims=True)
            o = o + A * pltpu.roll(v, d, 1)
    else:
        for idx, val in enumerate((q, kk, v, f)):
            nd_ref[idx] = val
        group = SUBLANES * SUBLANES
        for n in range(n_batch):
            for g0 in range(0, tt, group):
                views = [[nd_ref[idx, n, pl.ds(g0 + pos, SUBLANES, stride=SUBLANES), :]
                          for pos in range(SUBLANES)] for idx in range(4)]
                qv, kv, vv, fv = views
                for tp in range(SUBLANES):
                    acc = jnp.sum(qv[tp] * kv[tp], axis=-1, keepdims=True) * vv[tp]
                    E = None
                    for sp in range(tp - 1, -1, -1):
                        E = fv[sp + 1] if E is None else E * fv[sp + 1]
                        A = jnp.sum(qv[tp] * kv[sp] * E, axis=-1, keepdims=True)
                        acc = acc + A * vv[sp]
                    nd_ref[4, n, pl.ds(g0 + tp, SUBLANES, stride=SUBLANES), :] = acc
            fill()
        o = o + nd_ref[4]

    if tt > SUBLANES:
        b_ref[...] = b
        A_far = None
        h = SUBLANES
        while h < tt:
            for j in range(tt // h):
                rows = slice(j * h, (j + 1) * h)
                if j % 2 == 1:
                    b_mid = b_ref[:, pl.ds(j * h - 1, 1), :]
                    qt_ref[:, rows, :] = q[:, rows, :] * jnp.exp(b[:, rows, :] - b_mid)
                    kt_ref[:, rows, :] = jnp.zeros((n_batch, h, DH), _F32)
                else:
                    b_mid = b_ref[:, pl.ds((j + 1) * h - 1, 1), :]
                    kt_ref[:, rows, :] = kk[:, rows, :] * jnp.exp(b_mid - b[:, rows, :])
                    qt_ref[:, rows, :] = jnp.zeros((n_batch, h, DH), _F32)
            A_h = _dot_nt(qt_ref[...].astype(_BF16), kt_ref[...].astype(_BF16))
            if 2 * h < tt:
                A_h = jnp.where(xor_rc < 2 * h, A_h, 0.0)
            A_far = A_h if A_far is None else A_far + A_h
            fill()
            h *= 2
        o = o + _dot(A_far.astype(_BF16), vb)

    bL = _row_at(b, row, t_last)
    valid = row <= t_last
    ks = jnp.where(valid, kk * jnp.exp(jnp.where(valid, bL - b, 0.0)), 0.0)
    bL_col = jnp.sum(jnp.where(eye_dh, bL, 0.0), axis=-1, keepdims=True)
    S_new = jnp.exp(bL_col) * S0 + _dot_tn(ks.astype(_BF16), vb)
    return o, S_new


def _headnorm(h, g):
    return h * lax.rsqrt(jnp.mean(h * h, axis=-1, keepdims=True) + EPS) * g


def _mixers(p_ref, g, convw_ref, convb_ref, mnorm_ref, lbl_ref, hnorm_ref,
            mix_ref, convout_ref, C_in, n_in, m_in, S_in, C_out, n_out, m_out, S_out,
            convbuf, b_ref, qt_ref, kt_ref, nd_ref, tt, t_last, fill=lambda: None):
    batched = len(p_ref.shape) == 3
    n_tiles = p_ref.shape[0] if batched else 1
    lead = (slice(None),) if batched else ()

    def rows(start, size):
        return lead + (pl.ds(start, size), slice(None))

    def cols(start, size):
        return lead + (slice(None), slice(start, start + size))

    def head(hd):
        return lead + (hd,)

    def head_row(hd):
        return lead + (slice(hd, hd + 1), slice(None))

    def cat_heads(get):
        parts = [get(hd) for hd in range(N_HEADS)]
        return jnp.concatenate([x if batched else x[None] for x in parts], axis=0)

    def head_cols(x):
        return cat_heads(lambda hd: x[..., hd * DH:(hd + 1) * DH])

    def head_gain(ref):
        return cat_heads(lambda hd: jnp.broadcast_to(ref[:, hd * DH:(hd + 1) * DH], (n_tiles, 1, DH))
                         if batched else ref[:, hd * DH:(hd + 1) * DH])

    def of_head(x, hd):
        return x[hd * n_tiles:(hd + 1) * n_tiles] if batched else x[hd]

    row = lax.broadcasted_iota(jnp.int32, (tt, 1), 0)
    ri = lax.broadcasted_iota(jnp.int32, (tt, tt), 0)
    ci = lax.broadcasted_iota(jnp.int32, (tt, tt), 1)
    tril = ci <= ri
    eye = ci == ri
    xor_rc = ri ^ ci
    rd = lax.broadcasted_iota(jnp.int32, (DH, DH), 0)
    cd = lax.broadcasted_iota(jnp.int32, (DH, DH), 1)
    eye_dh = rd == cd

    prev = CONV_PAD - (CONV_W - 1)
    last = CONV_PAD + t_last - (CONV_W - 2)
    convbuf[rows(CONV_PAD, tt)] = p_ref[cols(0, 2 * D_A)]
    conv = convb_ref[...]
    for j in range(CONV_W):
        conv = conv + convw_ref[j:j + 1, :] * convbuf[rows(prev + j, tt)]
    new_rows = convbuf[rows(last, CONV_W - 1)]
    convout_ref[...] = new_rows
    convbuf[rows(prev, CONV_W - 1)] = new_rows
    fill()
    qk = conv * _sigmoid(conv)
    fill()
    q = head_cols(qk[..., 0:D_A]) * (DH ** -0.5)
    k = head_cols(qk[..., D_A:2 * D_A])

    base = 4 * D_A
    f_raw = p_ref[cols(base + D_B, D_B)]
    logits = lbl_ref[...]
    ex = jnp.exp(logits - jnp.max(logits, axis=0, keepdims=True))
    lb = ex[0:1, :] / jnp.sum(ex, axis=0, keepdims=True)
    f = lb + (1.0 - lb) * _sigmoid(f_raw)
    lf = jnp.log(f)

    lf_all = _log_sigmoid(g)
    if batched or tt == SUBLANES:
        b = _sublane_cumsum(lf, row)
        Fc, uc, ur = _gate_vectors_small(cat_heads(lambda hd: g[..., hd:hd + 1]),
                                         cat_heads(lambda hd: lf_all[..., N_HEADS + hd:N_HEADS + hd + 1]),
                                         tril, eye)
    else:
        tril_bf16 = jnp.where(tril, 1.0, 0.0).astype(_BF16)
        triu_bf16 = jnp.where(ci >= ri, 1.0, 0.0).astype(_BF16)
        cum = _tril_matmul_cumsum(jnp.concatenate([lf, lf_all], axis=1), tril_bf16)
        b = cum[:, 0:D_B]
        g_t = g.T
        F_rows = _triu_matmul_cumsum(_log_sigmoid(g_t[0:2 * N_HEADS, :]), triu_bf16)
        Fc = cat_heads(lambda hd: cum[:, D_B + N_HEADS + hd:D_B + N_HEADS + hd + 1])
        uc = cat_heads(lambda hd: g[:, hd:hd + 1]) - Fc
        ur = cat_heads(lambda hd: g_t[hd:hd + 1, :] - F_rows[N_HEADS + hd:N_HEADS + hd + 1, :])

    fill()
    v = head_cols(p_ref[cols(2 * D_A, D_A)])
    o_raw = head_cols(p_ref[cols(3 * D_A, D_A)])
    C0 = cat_heads(lambda hd: C_in[head(hd)])
    n0 = cat_heads(lambda hd: n_in[head_row(hd)])
    m0 = cat_heads(m_in)
    h, C_new, n_new, m_new = _mlstm_head(q, k, v, Fc, uc, ur, C0, n0, m0, tril, row, t_last, fill)
    out_a = _sigmoid(o_raw) * _headnorm(h, head_gain(mnorm_ref))
    m_new = jnp.broadcast_to(m_new, m_new.shape[:-1] + (GATE_COLS,))
    for hd in range(N_HEADS):
        C_out[head(hd)] = of_head(C_new, hd)
        n_out[head_row(hd)] = of_head(n_new, hd)
        m_out[head_row(hd)] = of_head(m_new, hd)
        mix_ref[cols(hd * DH, DH)] = of_head(out_a, hd).astype(mix_ref.dtype)

    q_raw = p_ref[cols(base, D_B)]
    g_raw = p_ref[cols(base + 3 * D_B, D_B)]
    kk = (1.0 - lb) * _sigmoid(-f_raw)
    qs = q_raw * _sigmoid(q_raw)
    S0 = cat_heads(lambda hd: S_in[head(hd)])
    o, S_new = _hgrn_core(head_cols(qs), head_cols(kk), head_cols(p_ref[cols(base + 2 * D_B, D_B)]),
                          head_cols(f), head_cols(b), S0, eye_dh, xor_rc, row, t_last,
                          b_ref, qt_ref, kt_ref, nd_ref, fill)
    out_b = _headnorm(o, head_gain(hnorm_ref)) * head_cols(g_raw * _sigmoid(g_raw))
    for hd in range(N_HEADS):
        S_out[head(hd)] = of_head(S_new, hd)
        mix_ref[cols(D_A + hd * DH, DH)] = of_head(out_b, hd).astype(mix_ref.dtype)


def _project(x, nm_ref, wa_ref, wb_ref, bmain_ref, wg_ref, bg_ref):
    h = _rmsnorm(x, nm_ref[...]).astype(_BF16)
    half = P_MAIN // 2
    p = jnp.concatenate([_dot_nt(h, wa_ref[...]) + bmain_ref[:, 0:half],
                         _dot_nt(h, wb_ref[...]) + bmain_ref[:, half:P_MAIN]], axis=1)
    g = _dot_nt(h, wg_ref[...]) + bg_ref[...]
    return p, g


def _layer_kernel(x_ref, xprev_ref, nm_ref, wa_ref, wb_ref, bmain_ref, wg_ref, bg_ref, convw_ref, convb_ref,
                  mnorm_ref, lbl_ref, hnorm_ref, wout_ref, nf_ref, wgate_ref, wup_ref, wdown_ref, nfin_ref,
                  y_ref, convp_ref, C_ref, n_ref, m_ref, S_ref,
                  p_scr, g_scr, mix_scr, convbuf, b_ref, qt_ref, kt_ref, nd_ref, n_tiles, tiles_per_row):
    tt = x_ref.shape[0]
    t = pl.program_id(0)
    ff_edges = (0, 768, 1536, 2176, D_FF)
    fill_stride = 2

    def ffn_pieces():
        held = {}

        def out_proj():
            x1 = xprev_ref[...] + _dot(mix_scr[...], wout_ref[...])
            held.update(x1=x1, h2=_rmsnorm(x1, nf_ref[...]).astype(_BF16))

        def gate(c):
            cs = pl.ds(ff_edges[c], ff_edges[c + 1] - ff_edges[c])

            def emit():
                held["gate"] = _dot(held["h2"], wgate_ref[:, cs])
            return emit

        def up_down(c):
            cs = pl.ds(ff_edges[c], ff_edges[c + 1] - ff_edges[c])

            def emit():
                gt = held["gate"]
                ff = (gt * _sigmoid(gt) * _dot(held["h2"], wup_ref[:, cs])).astype(_BF16)
                part = _dot(ff, wdown_ref[cs, :])
                held["x2"] = (held["x1"] if c == 0 else held["x2"]) + part
            return emit

        def final():
            y_ref[...] = _rmsnorm(held["x2"], nfin_ref[...])

        pieces = [out_proj]
        for c in range(len(ff_edges) - 1):
            pieces += [gate(c), up_down(c)]
        return pieces + [final]

    @pl.when(t == 0)
    def _():
        mix_scr[...] = jnp.zeros_like(mix_scr)

    @pl.when(jnp.logical_and(t < n_tiles, t % tiles_per_row == 0))
    def _():
        C_ref[...] = jnp.zeros_like(C_ref)
        n_ref[...] = jnp.zeros_like(n_ref)
        m_ref[...] = jnp.zeros_like(m_ref)
        S_ref[...] = jnp.zeros_like(S_ref)
        convbuf[0:CONV_PAD, :] = jnp.zeros((CONV_PAD, 2 * D_A), _F32)

    @pl.when(t < n_tiles)
    def _():
        pieces = ffn_pieces()
        calls = [0]

        def fill():
            if pieces and calls[0] % fill_stride == 0:
                pieces.pop(0)()
            calls[0] += 1

        fill()
        p, g = _project(x_ref[...], nm_ref, wa_ref, wb_ref, bmain_ref, wg_ref, bg_ref)
        p_scr[...] = p
        g_scr[...] = g
        ck = CHUNK_PROMPT
        for r0 in range(0, tt, ck):
            _mixers(p_scr.at[pl.ds(r0, ck)], g_scr[pl.ds(r0, ck), :], convw_ref, convb_ref, mnorm_ref, lbl_ref,
                    hnorm_ref, mix_scr.at[pl.ds(r0, ck)], convp_ref, C_ref, n_ref,
                    lambda hd: m_ref[hd:hd + 1, 0:1], S_ref, C_ref, n_ref, m_ref, S_ref,
                    convbuf, b_ref, qt_ref, kt_ref, nd_ref, ck, ck - 1, fill)
        while pieces:
            pieces.pop(0)()

    @pl.when(t == n_tiles)
    def _():
        for piece in ffn_pieces():
            piece()


def _proj_kernel(x_ref, nm_ref, wa_ref, wb_ref, bmain_ref, wg_ref, bg_ref, p_ref, g_ref):
    p, g = _project(x_ref[...], nm_ref, wa_ref, wb_ref, bmain_ref, wg_ref, bg_ref)
    p_ref[...] = p
    g_ref[...] = g


def _sample_kernel(p_ref, g_ref, conv0_ref, C0_ref, n0_ref, m0_ref, S0_ref,
                   convw_ref, convb_ref, mnorm_ref, lbl_ref, hnorm_ref,
                   mix_ref, convs_ref, C_ref, n_ref, m_ref, S_ref,
                   convbuf, t_last):
    n_seq, tt, _ = p_ref.shape
    lead_rows = CONV_PAD - (CONV_W - 1)
    convbuf[:, 0:lead_rows, :] = jnp.zeros((n_seq, lead_rows, 2 * D_A), _F32)
    convbuf[:, lead_rows:CONV_PAD, :] = conv0_ref[...]
    m_ref[...] = jnp.zeros_like(m_ref)
    _mixers(p_ref, g_ref[...], convw_ref, convb_ref, mnorm_ref, lbl_ref, hnorm_ref,
            mix_ref, convs_ref, C0_ref, n0_ref, lambda hd: m0_ref[:, hd:hd + 1, :], S0_ref,
            C_ref, n_ref, m_ref, S_ref, convbuf, None, None, None, None, tt, t_last)


def _ffn_kernel(x_ref, mix_ref, wout_ref, nf_ref, wgate_ref, wup_ref, wdown_ref, nfin_ref, y_ref):
    x1 = x_ref[...] + _dot(mix_ref[...], wout_ref[...])
    h2 = _rmsnorm(x1, nf_ref[...]).astype(_BF16)
    gate = _dot(h2, wgate_ref[...])
    up = _dot(h2, wup_ref[...])
    ff = (gate * _sigmoid(gate) * up).astype(_BF16)
    x2 = x1 + _dot(ff, wdown_ref[...])
    y_ref[...] = _rmsnorm(x2, nfin_ref[...])


def _resident(shape):
    nd = len(shape)
    return pl.BlockSpec(shape, lambda *_: (0,) * nd, pipeline_mode=pl.Buffered(1))


def _prompt_layer(x, mixer_w, ffn_w):
    B, T, D = x.shape
    tt = TT_PROMPT
    ck = CHUNK_PROMPT
    n_tiles = B * T // tt
    tiles_per_row = T // tt
    x2 = x.reshape(B * T, D)
    weights = tuple(mixer_w) + tuple(ffn_w)

    def cur(t):
        return jnp.minimum(t, n_tiles - 1)

    def prev(t):
        return jnp.maximum(t - 1, 0)

    def row(t):
        return cur(t) // tiles_per_row

    out_shape = (
        jax.ShapeDtypeStruct((B * T, D), _F32),
        jax.ShapeDtypeStruct((B, CONV_W - 1, 2 * D_A), _F32),
        jax.ShapeDtypeStruct((B, N_HEADS, DH, DH), _F32),
        jax.ShapeDtypeStruct((B, N_HEADS, DH), _F32),
        jax.ShapeDtypeStruct((B, SUBLANES, GATE_COLS), _F32),
        jax.ShapeDtypeStruct((B, N_HEADS, DH, DH), _F32),
    )
    in_specs = [
        pl.BlockSpec((tt, D), lambda t: (cur(t), 0)),
        pl.BlockSpec((tt, D), lambda t: (prev(t), 0)),
    ] + [_resident(w.shape) for w in weights]
    out_specs = (
        pl.BlockSpec((tt, D), lambda t: (prev(t), 0)),
        pl.BlockSpec((None, CONV_W - 1, 2 * D_A), lambda t: (row(t), 0, 0)),
        pl.BlockSpec((None, N_HEADS, DH, DH), lambda t: (row(t), 0, 0, 0)),
        pl.BlockSpec((None, N_HEADS, DH), lambda t: (row(t), 0, 0)),
        pl.BlockSpec((None, SUBLANES, GATE_COLS), lambda t: (row(t), 0, 0)),
        pl.BlockSpec((None, N_HEADS, DH, DH), lambda t: (row(t), 0, 0, 0)),
    )
    scratch = [
        pltpu.VMEM((tt, P_MAIN), _F32), pltpu.VMEM((tt, GATE_COLS), _F32),
        pltpu.VMEM((tt, D_MODEL), _BF16),
        pltpu.VMEM((CONV_PAD + ck, 2 * D_A), _F32),
        pltpu.VMEM((N_HEADS, ck, DH), _F32),
        pltpu.VMEM((N_HEADS, ck, DH), _F32),
        pltpu.VMEM((N_HEADS, ck, DH), _F32),
        pltpu.VMEM((5, N_HEADS, ck, DH), _F32),
    ]
    return pl.pallas_call(
        functools.partial(_layer_kernel, n_tiles=n_tiles, tiles_per_row=tiles_per_row),
        out_shape=out_shape, grid=(n_tiles + 1,), in_specs=in_specs, out_specs=out_specs,
        scratch_shapes=scratch, name="prompt_layer",
        compiler_params=pltpu.CompilerParams(dimension_semantics=("arbitrary",), vmem_limit_bytes=VMEM_LIMIT),
    )(x2, x2, *weights)


def _sample_proj(x, nm, wa, wb, bmain, wg, bg):
    N, D = x.shape
    tm = min(N, 512)
    return pl.pallas_call(
        _proj_kernel,
        out_shape=(jax.ShapeDtypeStruct((N, P_MAIN), _F32), jax.ShapeDtypeStruct((N, GATE_COLS), _F32)),
        grid=(N // tm,),
        in_specs=[pl.BlockSpec((tm, D), lambda i: (i, 0))] + [_resident(a.shape) for a in (nm, wa, wb, bmain, wg, bg)],
        out_specs=(pl.BlockSpec((tm, P_MAIN), lambda i: (i, 0)), pl.BlockSpec((tm, GATE_COLS), lambda i: (i, 0))),
        name="sample_proj",
        compiler_params=pltpu.CompilerParams(dimension_semantics=("arbitrary",), vmem_limit_bytes=VMEM_LIMIT),
    )(x, nm, wa, wb, bmain, wg, bg)


def _sample_mixer(p, g, conv0, C0, n0, m0, S0, convw, convb, mnorm, lbl, hnorm, t_last):
    B, tt, _ = p.shape
    out_shape = (
        jax.ShapeDtypeStruct((B, tt, D_MODEL), _BF16),
        jax.ShapeDtypeStruct((B, CONV_W - 1, 2 * D_A), _F32),
        jax.ShapeDtypeStruct((B, N_HEADS, DH, DH), _F32),
        jax.ShapeDtypeStruct((B, N_HEADS, DH), _F32),
        jax.ShapeDtypeStruct((B, SUBLANES, GATE_COLS), _F32),
        jax.ShapeDtypeStruct((B, N_HEADS, DH, DH), _F32),
    )
    ns = SEQS_PER_STEP
    in_specs = [
        pl.BlockSpec((ns, tt, P_MAIN), lambda b: (b, 0, 0)),
        pl.BlockSpec((ns, tt, GATE_COLS), lambda b: (b, 0, 0)),
        pl.BlockSpec((ns, CONV_W - 1, 2 * D_A), lambda b: (b, 0, 0)),
        pl.BlockSpec((ns, N_HEADS, DH, DH), lambda b: (b, 0, 0, 0)),
        pl.BlockSpec((ns, N_HEADS, DH), lambda b: (b, 0, 0)),
        pl.BlockSpec((ns, N_HEADS, 1), lambda b: (b, 0, 0)),
        pl.BlockSpec((ns, N_HEADS, DH, DH), lambda b: (b, 0, 0, 0)),
        _resident(convw.shape), _resident(convb.shape), _resident(mnorm.shape), _resident(lbl.shape),
        _resident(hnorm.shape),
    ]
    out_specs = (
        pl.BlockSpec((ns, tt, D_MODEL), lambda b: (b, 0, 0)),
        pl.BlockSpec((ns, CONV_W - 1, 2 * D_A), lambda b: (b, 0, 0)),
        pl.BlockSpec((ns, N_HEADS, DH, DH), lambda b: (b, 0, 0, 0)),
        pl.BlockSpec((ns, N_HEADS, DH), lambda b: (b, 0, 0)),
        pl.BlockSpec((ns, SUBLANES, GATE_COLS), lambda b: (b, 0, 0)),
        pl.BlockSpec((ns, N_HEADS, DH, DH), lambda b: (b, 0, 0, 0)),
    )
    return pl.pallas_call(
        functools.partial(_sample_kernel, t_last=t_last),
        out_shape=out_shape, grid=(B // ns,), in_specs=in_specs, out_specs=out_specs,
        scratch_shapes=[pltpu.VMEM((ns, CONV_PAD + tt, 2 * D_A), _F32)], name="sample_mixer",
        compiler_params=pltpu.CompilerParams(dimension_semantics=("arbitrary",), vmem_limit_bytes=VMEM_LIMIT),
    )(p, g, conv0, C0, n0, m0, S0, convw, convb, mnorm, lbl, hnorm)


def _ffn(x, mix, wout, nf, wgate, wup, wdown, nfin):
    N, D = x.shape
    tm = min(N, TM_FFN)
    return pl.pallas_call(
        _ffn_kernel, out_shape=jax.ShapeDtypeStruct((N, D), _F32), grid=(N // tm,),
        in_specs=[pl.BlockSpec((tm, D), lambda i: (i, 0)), pl.BlockSpec((tm, D_MODEL), lambda i: (i, 0)),
                  _resident(wout.shape), _resident(nf.shape), _resident(wgate.shape), _resident(wup.shape),
                  _resident(wdown.shape), _resident(nfin.shape)],
        out_specs=pl.BlockSpec((tm, D), lambda i: (i, 0)), name="ffn",
        compiler_params=pltpu.CompilerParams(dimension_semantics=("arbitrary",), vmem_limit_bytes=VMEM_LIMIT),
    )(x, mix, wout, nf, wgate, wup, wdown, nfin)


def kernel(x_prompt, x_sample, state_conv, state_mlstm_C, state_mlstm_n, state_mlstm_m, state_hgrn_S,
           norm_mix, w_in, b_in, conv_w, conv_b, mlstm_norm, hgrn_lb_logits, hgrn_norm, w_out,
           norm_ffn, w_gate, w_up, w_down, norm_final):
    depth = w_in.shape[0]
    assert depth == 1, "single-layer stack"
    B, T, D = x_prompt.shape
    BS, TS, _ = x_sample.shape
    l = 0
    g0 = 4 * D_A
    n_gates = 2 * N_HEADS

    wt = jnp.swapaxes(w_in[l], 0, 1)
    wa = wt[:g0].astype(_BF16)
    wb = wt[g0 + n_gates:].astype(_BF16)
    wg = jnp.pad(wt[g0:g0 + n_gates], ((0, GATE_COLS - n_gates), (0, 0))).astype(_BF16)
    bmain = jnp.concatenate([b_in[l, :g0], b_in[l, g0 + n_gates:]])[None, :]
    bg = jnp.pad(b_in[l, g0:g0 + n_gates], (0, GATE_COLS - n_gates))[None, :]
    nm = norm_mix[l][None, :]
    convw = conv_w[l]
    convb = conv_b[l][None, :]
    mnorm = mlstm_norm[l][None, :]
    hnorm = hgrn_norm[l][None, :]
    lbl = hgrn_lb_logits.astype(_F32)
    wout = w_out[l].astype(_BF16)
    wgate = w_gate[l].astype(_BF16)
    wup = w_up[l].astype(_BF16)
    wdown = w_down[l].astype(_BF16)
    nf = norm_ffn[l][None, :]
    nfin = norm_final[None, :]

    mixer_w = (nm, wa, wb, bmain, wg, bg, convw, convb, mnorm, lbl, hnorm)
    ffn_w = (wout, nf, wgate, wup, wdown, nfin)
    y_p, conv_p, C_p, n_p, m_p, S_p = _prompt_layer(x_prompt, mixer_w, ffn_w)
    y_prompt = y_p.reshape(B, T, D)

    xs_pad = jnp.pad(x_sample, ((0, 0), (0, TT_SAMPLE - TS), (0, 0)))
    p_s, g_s = _sample_proj(xs_pad.reshape(BS * TT_SAMPLE, D), nm, wa, wb, bmain, wg, bg)
    mix_s, conv_s, C_s, n_s, m_s, S_s = _sample_mixer(
        p_s.reshape(BS, TT_SAMPLE, P_MAIN), g_s.reshape(BS, TT_SAMPLE, GATE_COLS), state_conv[l],
        state_mlstm_C[l], state_mlstm_n[l], state_mlstm_m[l][:, :, None], state_hgrn_S[l],
        convw, convb, mnorm, lbl, hnorm, TS - 1)
    y_sample = _ffn(x_sample.reshape(BS * TS, D), mix_s[:, :TS, :].reshape(BS * TS, D_MODEL), wout, nf, wgate,
                    wup, wdown, nfin).reshape(BS, TS, D)

    return (y_prompt, y_sample,
            conv_p[None], C_p[None], n_p[None], m_p[None, :, :N_HEADS, 0], S_p[None],
            conv_s[None], C_s[None], n_s[None], m_s[None, :, :N_HEADS, 0], S_s[None])
```

```python
import functools

import jax
import jax.numpy as jnp
from jax import lax
from jax.experimental import pallas as pl
from jax.experimental.pallas import tpu as pltpu

D_MODEL = 1024
D_A = 512
D_B = 512
N_HEADS = 4
DH = 128
CONV_W = 4
D_FF = 2816
EPS = 1e-6
P_MAIN = 2 * D_A + D_A + D_A + 4 * D_B
GATE_COLS = 128
SUBLANES = 8
CONV_PAD = 8
VMEM_LIMIT = 56 * 1024 * 1024

TT_PROMPT = 256
CHUNK_PROMPT = 256
TT_SAMPLE = 8
TM_FFN = 512
SEQS_PER_STEP = 16

_F32 = jnp.float32
_BF16 = jnp.bfloat16


def _dot_dims(a, b, ca, cb):
    batch = ((0,), (0,)) if a.ndim == 3 and b.ndim == 3 else ((), ())
    return lax.dot_general(a, b, (((a.ndim + ca,), (b.ndim + cb,)), batch), preferred_element_type=_F32)


def _dot(a, b):
    return _dot_dims(a, b, -1, -2)


def _dot_nt(a, b):
    return _dot_dims(a, b, -1, -1)


def _dot_tn(a, b):
    return _dot_dims(a, b, -2, -2)


def _sigmoid(x):
    return 1.0 / (1.0 + jnp.exp(-x))


def _rmsnorm(x, g):
    return x * lax.rsqrt(jnp.mean(x * x, axis=-1, keepdims=True) + EPS) * g


def _bf16_pieces(x):
    hi = x.astype(_BF16)
    r1 = x - hi.astype(_F32)
    mid = r1.astype(_BF16)
    lo = (r1 - mid.astype(_F32)).astype(_BF16)
    return hi, mid, lo


def _tril_matmul_cumsum(x, tril_bf16):
    hi, mid, lo = _bf16_pieces(x)
    return _dot(tril_bf16, hi) + _dot(tril_bf16, mid) + _dot(tril_bf16, lo)


def _triu_matmul_cumsum(x, triu_bf16):
    hi, mid, lo = _bf16_pieces(x)
    return _dot(hi, triu_bf16) + _dot(mid, triu_bf16) + _dot(lo, triu_bf16)


def _log_sigmoid(x):
    return jnp.minimum(x, 0.0) - jnp.log(1.0 + jnp.exp(-jnp.abs(x)))


def _sublane_cumsum(x, row):
    for sh in (1, 2, 4):
        x = x + jnp.where(row >= sh, pltpu.roll(x, sh, x.ndim - 2), 0.0)
    return x


def _row_at(x, row, t):
    return jnp.sum(jnp.where(row == t, x, 0.0), axis=-2, keepdims=True)


def _gate_vectors_small(igc, lfc, tril, eye):
    lfr = jnp.sum(jnp.where(eye, lfc, 0.0), axis=-2, keepdims=True)
    Fc = jnp.sum(jnp.where(tril, lfr, 0.0), axis=-1, keepdims=True)
    uc = igc - Fc
    ur = jnp.sum(jnp.where(eye, uc, 0.0), axis=-2, keepdims=True)
    return Fc, uc, ur


def _mlstm_head(q, k, v, Fc, uc, ur, C0, n0, m0, tril, row, t_last, fill):
    a = Fc + m0
    D = jnp.where(tril, Fc + ur, -jnp.inf)
    mt = jnp.maximum(a, jnp.max(D, axis=-1, keepdims=True))
    w_inter = jnp.exp(a - mt)
    W = jnp.exp(D - mt)
    fill()
    qb = q.astype(_BF16)
    kb = k.astype(_BF16)
    vb = v.astype(_BF16)
    s = _dot_nt(qb, kb) * W
    fill()
    num = w_inter * _dot(qb, C0.astype(_BF16)) + _dot(s.astype(_BF16), vb)
    den = w_inter * jnp.sum(q * n0, axis=-1, keepdims=True) + jnp.sum(s, axis=-1, keepdims=True)
    h = num / jnp.maximum(jnp.abs(den), jnp.exp(-mt))
    fill()
    F_last = _row_at(Fc, row, t_last)
    m_last = _row_at(mt, row, t_last)
    decay = _row_at(w_inter, row, t_last)
    valid = row <= t_last
    wl = jnp.where(valid, jnp.exp(jnp.where(valid, F_last + uc - m_last, 0.0)), 0.0)
    kw = wl * k
    C_new = decay * C0 + _dot_tn(kw.astype(_BF16), vb)
    n_new = decay * n0 + jnp.sum(kw, axis=-2, keepdims=True)
    return h, C_new, n_new, m_last


def _hgrn_core(q, kk, v, f, b, S0, eye_dh, xor_rc, row, t_last, b_ref, qt_ref, kt_ref, nd_ref, fill):
    n_batch, tt, _ = q.shape
    vb = v.astype(_BF16)
    o = _dot((q * jnp.exp(b)).astype(_BF16), S0.astype(_BF16))
    fill()

    if nd_ref is None:
        fz = jnp.where((row & (SUBLANES - 1)) == 0, 0.0, f)
        o = o + jnp.sum(q * kk, axis=-1, keepdims=True) * v
        E = fz
        for d in range(1, SUBLANES):
            if d > 1:
                E = E * pltpu.roll(fz, d - 1, 1)
            A = jnp.sum(q * pltpu.roll(kk, d, 1) * E, axis=-1, keepdims=True)
            o = o + A * pltpu.roll(v, d, 1)
    else:
        for idx, val in enumerate((q, kk, v, f)):
            nd_ref[idx] = val
        group = SUBLANES * SUBLANES
        for n in range(n_batch):
            for g0 in range(0, tt, group):
                views = [[nd_ref[idx, n, pl.ds(g0 + pos, SUBLANES, stride=SUBLANES), :]
                          for pos in range(SUBLANES)] for idx in range(4)]
                qv, kv, vv, fv = views
                for tp in range(SUBLANES):
                    acc = jnp.sum(qv[tp] * kv[tp], axis=-1, keepdims=True) * vv[tp]
                    E = None
                    for sp in range(tp - 1, -1, -1):
                        E = fv[sp + 1] if E is None else E * fv[sp + 1]
                        A = jnp.sum(qv[tp] * kv[sp] * E, axis=-1, keepdims=True)
                        acc = acc + A * vv[sp]
                    nd_ref[4, n, pl.ds(g0 + tp, SUBLANES, stride=SUBLANES), :] = acc
            fill()
        o = o + nd_ref[4]

    if tt > SUBLANES:
        b_ref[...] = b
        A_far = None
        h = SUBLANES
        while h < tt:
            for j in range(tt // h):
                rows = slice(j * h, (j + 1) * h)
                if j % 2 == 1:
                    b_mid = b_ref[:, pl.ds(j * h - 1, 1), :]
                    qt_ref[:, rows, :] = q[:, rows, :] * jnp.exp(b[:, rows, :] - b_mid)
                    kt_ref[:, rows, :] = jnp.zeros((n_batch, h, DH), _F32)
                else:
                    b_mid = b_ref[:, pl.ds((j + 1) * h - 1, 1), :]
                    kt_ref[:, rows, :] = kk[:, rows, :] * jnp.exp(b_mid - b[:, rows, :])
                    qt_ref[:, rows, :] = jnp.zeros((n_batch, h, DH), _F32)
            A_h = _dot_nt(qt_ref[...].astype(_BF16), kt_ref[...].astype(_BF16))
            if 2 * h < tt:
                A_h = jnp.where(xor_rc < 2 * h, A_h, 0.0)
            A_far = A_h if A_far is None else A_far + A_h
            fill()
            h *= 2
        o = o + _dot(A_far.astype(_BF16), vb)

    bL = _row_at(b, row, t_last)
    valid = row <= t_last
    ks = jnp.where(valid, kk * jnp.exp(jnp.where(valid, bL - b, 0.0)), 0.0)
    bL_col = jnp.sum(jnp.where(eye_dh, bL, 0.0), axis=-1, keepdims=True)
    S_new = jnp.exp(bL_col) * S0 + _dot_tn(ks.astype(_BF16), vb)
    return o, S_new


def _headnorm(h, g):
    return h * lax.rsqrt(jnp.mean(h * h, axis=-1, keepdims=True) + EPS) * g


def _mixers(p_ref, g, convw_ref, convb_ref, mnorm_ref, lbl_ref, hnorm_ref,
            mix_ref, convout_ref, C_in, n_in, m_in, S_in, C_out, n_out, m_out, S_out,
            convbuf, b_ref, qt_ref, kt_ref, nd_ref, tt, t_last, fill=lambda: None):
    batched = len(p_ref.shape) == 3
    n_tiles = p_ref.shape[0] if batched else 1
    lead = (slice(None),) if batched else ()

    def rows(start, size):
        return lead + (pl.ds(start, size), slice(None))

    def cols(start, size):
        return lead + (slice(None), slice(start, start + size))

    def head(hd):
        return lead + (hd,)

    def head_row(hd):
        return lead + (slice(hd, hd + 1), slice(None))

    def cat_heads(get):
        parts = [get(hd) for hd in range(N_HEADS)]
        return jnp.concatenate([x if batched else x[None] for x in parts], axis=0)

    def head_cols(x):
        return cat_heads(lambda hd: x[..., hd * DH:(hd + 1) * DH])

    def head_gain(ref):
        return cat_heads(lambda hd: jnp.broadcast_to(ref[:, hd * DH:(hd + 1) * DH], (n_tiles, 1, DH))
                         if batched else ref[:, hd * DH:(hd + 1) * DH])

    def of_head(x, hd):
        return x[hd * n_tiles:(hd + 1) * n_tiles] if batched else x[hd]

    row = lax.broadcasted_iota(jnp.int32, (tt, 1), 0)
    ri = lax.broadcasted_iota(jnp.int32, (tt, tt), 0)
    ci = lax.broadcasted_iota(jnp.int32, (tt, tt), 1)
    tril = ci <= ri
    eye = ci == ri
    xor_rc = ri ^ ci
    rd = lax.broadcasted_iota(jnp.int32, (DH, DH), 0)
    cd = lax.broadcasted_iota(jnp.int32, (DH, DH), 1)
    eye_dh = rd == cd

    prev = CONV_PAD - (CONV_W - 1)
    last = CONV_PAD + t_last - (CONV_W - 2)
    convbuf[rows(CONV_PAD, tt)] = p_ref[cols(0, 2 * D_A)]
    conv = convb_ref[...]
    for j in range(CONV_W):
        conv = conv + convw_ref[j:j + 1, :] * convbuf[rows(prev + j, tt)]
    new_rows = convbuf[rows(last, CONV_W - 1)]
    convout_ref[...] = new_rows
    convbuf[rows(prev, CONV_W - 1)] = new_rows
    fill()
    qk = conv * _sigmoid(conv)
    fill()
    q = head_cols(qk[..., 0:D_A]) * (DH ** -0.5)
    k = head_cols(qk[..., D_A:2 * D_A])

    base = 4 * D_A
    f_raw = p_ref[cols(base + D_B, D_B)]
    logits = lbl_ref[...]
    ex = jnp.exp(logits - jnp.max(logits, axis=0, keepdims=True))
    lb = ex[0:1, :] / jnp.sum(ex, axis=0, keepdims=True)
    f = lb + (1.0 - lb) * _sigmoid(f_raw)
    lf = jnp.log(f)

    lf_all = _log_sigmoid(g)
    if batched or tt == SUBLANES:
        b = _sublane_cumsum(lf, row)
        Fc, uc, ur = _gate_vectors_small(cat_heads(lambda hd: g[..., hd:hd + 1]),
                                         cat_heads(lambda hd: lf_all[..., N_HEADS + hd:N_HEADS + hd + 1]),
                                         tril, eye)
    else:
        tril_bf16 = jnp.where(tril, 1.0, 0.0).astype(_BF16)
        triu_bf16 = jnp.where(ci >= ri, 1.0, 0.0).astype(_BF16)
        cum = _tril_matmul_cumsum(jnp.concatenate([lf, lf_all], axis=1), tril_bf16)
        b = cum[:, 0:D_B]
        g_t = g.T
        F_rows = _triu_matmul_cumsum(_log_sigmoid(g_t[0:2 * N_HEADS, :]), triu_bf16)
        Fc = cat_heads(lambda hd: cum[:, D_B + N_HEADS + hd:D_B + N_HEADS + hd + 1])
        uc = cat_heads(lambda hd: g[:, hd:hd + 1]) - Fc
        ur = cat_heads(lambda hd: g_t[hd:hd + 1, :] - F_rows[N_HEADS + hd:N_HEADS + hd + 1, :])

    fill()
    v = head_cols(p_ref[cols(2 * D_A, D_A)])
    o_raw = head_cols(p_ref[cols(3 * D_A, D_A)])
    C0 = cat_heads(lambda hd: C_in[head(hd)])
    n0 = cat_heads(lambda hd: n_in[head_row(hd)])
    m0 = cat_heads(m_in)
    h, C_new, n_new, m_new = _mlstm_head(q, k, v, Fc, uc, ur, C0, n0, m0, tril, row, t_last, fill)
    out_a = _sigmoid(o_raw) * _headnorm(h, head_gain(mnorm_ref))
    m_new = jnp.broadcast_to(m_new, m_new.shape[:-1] + (GATE_COLS,))
    for hd in range(N_HEADS):
        C_out[head(hd)] = of_head(C_new, hd)
        n_out[head_row(hd)] = of_head(n_new, hd)
        m_out[head_row(hd)] = of_head(m_new, hd)
        mix_ref[cols(hd * DH, DH)] = of_head(out_a, hd).astype(mix_ref.dtype)

    q_raw = p_ref[cols(base, D_B)]
    g_raw = p_ref[cols(base + 3 * D_B, D_B)]
    kk = (1.0 - lb) * _sigmoid(-f_raw)
    qs = q_raw * _sigmoid(q_raw)
    S0 = cat_heads(lambda hd: S_in[head(hd)])
    o, S_new = _hgrn_core(head_cols(qs), head_cols(kk), head_cols(p_ref[cols(base + 2 * D_B, D_B)]),
                          head_cols(f), head_cols(b), S0, eye_dh, xor_rc, row, t_last,
                          b_ref, qt_ref, kt_ref, nd_ref, fill)
    out_b = _headnorm(o, head_gain(hnorm_ref)) * head_cols(g_raw * _sigmoid(g_raw))
    for hd in range(N_HEADS):
        S_out[head(hd)] = of_head(S_new, hd)
        mix_ref[cols(D_A + hd * DH, DH)] = of_head(out_b, hd).astype(mix_ref.dtype)


def _project(x, nm_ref, wa_ref, wb_ref, bmain_ref, wg_ref, bg_ref):
    h = _rmsnorm(x, nm_ref[...]).astype(_BF16)
    half = P_MAIN // 2
    p = jnp.concatenate([_dot_nt(h, wa_ref[...]) + bmain_ref[:, 0:half],
                         _dot_nt(h, wb_ref[...]) + bmain_ref[:, half:P_MAIN]], axis=1)
    g = _dot_nt(h, wg_ref[...]) + bg_ref[...]
    return p, g


def _layer_kernel(xnext_ref, xprev_ref, nm_ref, wa_ref, wb_ref, bmain_ref, wg_ref, bg_ref, convw_ref, convb_ref,
                  mnorm_ref, lbl_ref, hnorm_ref, wout_ref, nf_ref, wgate_ref, wup_ref, wdown_ref, nfin_ref,
                  y_ref, convp_ref, C_ref, n_ref, m_ref, S_ref,
                  p_a, g_a, p_b, g_b, mix_scr, convbuf, b_ref, qt_ref, kt_ref, nd_ref, n_tiles, tiles_per_row):
    tt = xnext_ref.shape[0]
    t = pl.program_id(0)
    ff_edges = (0, 768, 1536, 2176, D_FF)
    n_chunk = 8
    cw = P_MAIN // n_chunk

    def projection_pieces(x_ref, p_scr, g_scr):
        held = {}

        def norm():
            held["h"] = _rmsnorm(x_ref[...], nm_ref[...]).astype(_BF16)

        def chunk(c):
            w_ref = wa_ref if c < n_chunk // 2 else wb_ref
            r0 = (c % (n_chunk // 2)) * cw

            def emit():
                p_scr[:, c * cw:(c + 1) * cw] = (_dot_nt(held["h"], w_ref[pl.ds(r0, cw), :])
                                                 + bmain_ref[:, c * cw:(c + 1) * cw])
            return emit

        def gates():
            g_scr[...] = _dot_nt(held["h"], wg_ref[...]) + bg_ref[...]

        return [norm] + [chunk(c) for c in range(n_chunk)] + [gates]

    def ffn_pieces():
        held = {}

        def out_proj():
            x1 = xprev_ref[...] + _dot(mix_scr[...], wout_ref[...])
            held.update(x1=x1, h2=_rmsnorm(x1, nf_ref[...]).astype(_BF16))

        def gate(c):
            cs = pl.ds(ff_edges[c], ff_edges[c + 1] - ff_edges[c])

            def emit():
                held["gate"] = _dot(held["h2"], wgate_ref[:, cs])
            return emit

        def up_down(c):
            cs = pl.ds(ff_edges[c], ff_edges[c + 1] - ff_edges[c])

            def emit():
                gt = held["gate"]
                ff = (gt * _sigmoid(gt) * _dot(held["h2"], wup_ref[:, cs])).astype(_BF16)
                part = _dot(ff, wdown_ref[cs, :])
                held["x2"] = (held["x1"] if c == 0 else held["x2"]) + part
            return emit

        def final():
            y_ref[...] = _rmsnorm(held["x2"], nfin_ref[...])

        pieces = [out_proj]
        for c in range(len(ff_edges) - 1):
            pieces += [gate(c), up_down(c)]
        return pieces + [final]

    def step(p_cur, g_cur, p_nxt, g_nxt):
        ffn = ffn_pieces()
        proj = projection_pieces(xnext_ref, p_nxt, g_nxt)
        ffn.pop(0)()
        pieces = [piece for pair in zip(proj, ffn) for piece in pair] + proj[len(ffn):] + ffn[len(proj):]

        def fill():
            if pieces:
                pieces.pop(0)()

        ck = CHUNK_PROMPT
        for r0 in range(0, tt, ck):
            _mixers(p_cur.at[pl.ds(r0, ck)], g_cur[pl.ds(r0, ck), :], convw_ref, convb_ref, mnorm_ref, lbl_ref,
                    hnorm_ref, mix_scr.at[pl.ds(r0, ck)], convp_ref, C_ref, n_ref,
                    lambda hd: m_ref[hd:hd + 1, 0:1], S_ref, C_ref, n_ref, m_ref, S_ref,
                    convbuf, b_ref, qt_ref, kt_ref, nd_ref, ck, ck - 1, fill)
        while pieces:
            fill()

    @pl.when(t == 0)
    def _():
        mix_scr[...] = jnp.zeros_like(mix_scr)
        for piece in projection_pieces(xprev_ref, p_a, g_a):
            piece()

    @pl.when(jnp.logical_and(t < n_tiles, t % tiles_per_row == 0))
    def _():
        C_ref[...] = jnp.zeros_like(C_ref)
        n_ref[...] = jnp.zeros_like(n_ref)
        m_ref[...] = jnp.zeros_like(m_ref)
        S_ref[...] = jnp.zeros_like(S_ref)
        convbuf[0:CONV_PAD, :] = jnp.zeros((CONV_PAD, 2 * D_A), _F32)

    @pl.when(jnp.logical_and(t < n_tiles, t % 2 == 0))
    def _():
        step(p_a, g_a, p_b, g_b)

    @pl.when(jnp.logical_and(t < n_tiles, t % 2 == 1))
    def _():
        step(p_b, g_b, p_a, g_a)

    @pl.when(t == n_tiles)
    def _():
        for piece in ffn_pieces():
            piece()


def _proj_kernel(x_ref, nm_ref, wa_ref, wb_ref, bmain_ref, wg_ref, bg_ref, p_ref, g_ref):
    p, g = _project(x_ref[...], nm_ref, wa_ref, wb_ref, bmain_ref, wg_ref, bg_ref)
    p_ref[...] = p
    g_ref[...] = g


def _sample_kernel(p_ref, g_ref, conv0_ref, C0_ref, n0_ref, m0_ref, S0_ref,
                   convw_ref, convb_ref, mnorm_ref, lbl_ref, hnorm_ref,
                   mix_ref, convs_ref, C_ref, n_ref, m_ref, S_ref,
                   convbuf, t_last):
    n_seq, tt, _ = p_ref.shape
    lead_rows = CONV_PAD - (CONV_W - 1)
    convbuf[:, 0:lead_rows, :] = jnp.zeros((n_seq, lead_rows, 2 * D_A), _F32)
    convbuf[:, lead_rows:CONV_PAD, :] = conv0_ref[...]
    m_ref[...] = jnp.zeros_like(m_ref)
    _mixers(p_ref, g_ref[...], convw_ref, convb_ref, mnorm_ref, lbl_ref, hnorm_ref,
            mix_ref, convs_ref, C0_ref, n0_ref, lambda hd: m0_ref[:, hd:hd + 1, :], S0_ref,
            C_ref, n_ref, m_ref, S_ref, convbuf, None, None, None, None, tt, t_last)


def _ffn_kernel(x_ref, mix_ref, wout_ref, nf_ref, wgate_ref, wup_ref, wdown_ref, nfin_ref, y_ref):
    x1 = x_ref[...] + _dot(mix_ref[...], wout_ref[...])
    h2 = _rmsnorm(x1, nf_ref[...]).astype(_BF16)
    gate = _dot(h2, wgate_ref[...])
    up = _dot(h2, wup_ref[...])
    ff = (gate * _sigmoid(gate) * up).astype(_BF16)
    x2 = x1 + _dot(ff, wdown_ref[...])
    y_ref[...] = _rmsnorm(x2, nfin_ref[...])


def _resident(shape):
    nd = len(shape)
    return pl.BlockSpec(shape, lambda *_: (0,) * nd, pipeline_mode=pl.Buffered(1))


def _prompt_layer(x, mixer_w, ffn_w):
    B, T, D = x.shape
    tt = TT_PROMPT
    ck = CHUNK_PROMPT
    n_tiles = B * T // tt
    tiles_per_row = T // tt
    x2 = x.reshape(B * T, D)
    weights = tuple(mixer_w) + tuple(ffn_w)

    def cur(t):
        return jnp.minimum(t, n_tiles - 1)

    def nxt(t):
        return jnp.minimum(t + 1, n_tiles - 1)

    def prev(t):
        return jnp.maximum(t - 1, 0)

    def row(t):
        return cur(t) // tiles_per_row

    out_shape = (
        jax.ShapeDtypeStruct((B * T, D), _F32),
        jax.ShapeDtypeStruct((B, CONV_W - 1, 2 * D_A), _F32),
        jax.ShapeDtypeStruct((B, N_HEADS, DH, DH), _F32),
        jax.ShapeDtypeStruct((B, N_HEADS, DH), _F32),
        jax.ShapeDtypeStruct((B, SUBLANES, GATE_COLS), _F32),
        jax.ShapeDtypeStruct((B, N_HEADS, DH, DH), _F32),
    )
    in_specs = [
        pl.BlockSpec((tt, D), lambda t: (nxt(t), 0)),
        pl.BlockSpec((tt, D), lambda t: (prev(t), 0)),
    ] + [_resident(w.shape) for w in weights]
    out_specs = (
        pl.BlockSpec((tt, D), lambda t: (prev(t), 0)),
        pl.BlockSpec((None, CONV_W - 1, 2 * D_A), lambda t: (row(t), 0, 0)),
        pl.BlockSpec((None, N_HEADS, DH, DH), lambda t: (row(t), 0, 0, 0)),
        pl.BlockSpec((None, N_HEADS, DH), lambda t: (row(t), 0, 0)),
        pl.BlockSpec((None, SUBLANES, GATE_COLS), lambda t: (row(t), 0, 0)),
        pl.BlockSpec((None, N_HEADS, DH, DH), lambda t: (row(t), 0, 0, 0)),
    )
    scratch = [
        pltpu.VMEM((tt, P_MAIN), _F32), pltpu.VMEM((tt, GATE_COLS), _F32),
        pltpu.VMEM((tt, P_MAIN), _F32), pltpu.VMEM((tt, GATE_COLS), _F32),
        pltpu.VMEM((tt, D_MODEL), _BF16),
        pltpu.VMEM((CONV_PAD + ck, 2 * D_A), _F32),
        pltpu.VMEM((N_HEADS, ck, DH), _F32),
        pltpu.VMEM((N_HEADS, ck, DH), _F32),
        pltpu.VMEM((N_HEADS, ck, DH), _F32),
        pltpu.VMEM((5, N_HEADS, ck, DH), _F32),
    ]
    return pl.pallas_call(
        functools.partial(_layer_kernel, n_tiles=n_tiles, tiles_per_row=tiles_per_row),
        out_shape=out_shape, grid=(n_tiles + 1,), in_specs=in_specs, out_specs=out_specs,
        scratch_shapes=scratch, name="prompt_layer",
        compiler_params=pltpu.CompilerParams(dimension_semantics=("arbitrary",), vmem_limit_bytes=VMEM_LIMIT),
    )(x2, x2, *weights)


def _sample_proj(x, nm, wa, wb, bmain, wg, bg):
    N, D = x.shape
    tm = min(N, 512)
    return pl.pallas_call(
        _proj_kernel,
        out_shape=(jax.ShapeDtypeStruct((N, P_MAIN), _F32), jax.ShapeDtypeStruct((N, GATE_COLS), _F32)),
        grid=(N // tm,),
        in_specs=[pl.BlockSpec((tm, D), lambda i: (i, 0))] + [_resident(a.shape) for a in (nm, wa, wb, bmain, wg, bg)],
        out_specs=(pl.BlockSpec((tm, P_MAIN), lambda i: (i, 0)), pl.BlockSpec((tm, GATE_COLS), lambda i: (i, 0))),
        name="sample_proj",
        compiler_params=pltpu.CompilerParams(dimension_semantics=("arbitrary",), vmem_limit_bytes=VMEM_LIMIT),
    )(x, nm, wa, wb, bmain, wg, bg)


def _sample_mixer(p, g, conv0, C0, n0, m0, S0, convw, convb, mnorm, lbl, hnorm, t_last):
    B, tt, _ = p.shape
    out_shape = (
        jax.ShapeDtypeStruct((B, tt, D_MODEL), _BF16),
        jax.ShapeDtypeStruct((B, CONV_W - 1, 2 * D_A), _F32),
        jax.ShapeDtypeStruct((B, N_HEADS, DH, DH), _F32),
        jax.ShapeDtypeStruct((B, N_HEADS, DH), _F32),
        jax.ShapeDtypeStruct((B, SUBLANES, GATE_COLS), _F32),
        jax.ShapeDtypeStruct((B, N_HEADS, DH, DH), _F32),
    )
    ns = SEQS_PER_STEP
    in_specs = [
        pl.BlockSpec((ns, tt, P_MAIN), lambda b: (b, 0, 0)),
        pl.BlockSpec((ns, tt, GATE_COLS), lambda b: (b, 0, 0)),
        pl.BlockSpec((ns, CONV_W - 1, 2 * D_A), lambda b: (b, 0, 0)),
        pl.BlockSpec((ns, N_HEADS, DH, DH), lambda b: (b, 0, 0, 0)),
        pl.BlockSpec((ns, N_HEADS, DH), lambda b: (b, 0, 0)),
        pl.BlockSpec((ns, N_HEADS, 1), lambda b: (b, 0, 0)),
        pl.BlockSpec((ns, N_HEADS, DH, DH), lambda b: (b, 0, 0, 0)),
        _resident(convw.shape), _resident(convb.shape), _resident(mnorm.shape), _resident(lbl.shape),
        _resident(hnorm.shape),
    ]
    out_specs = (
        pl.BlockSpec((ns, tt, D_MODEL), lambda b: (b, 0, 0)),
        pl.BlockSpec((ns, CONV_W - 1, 2 * D_A), lambda b: (b, 0, 0)),
        pl.BlockSpec((ns, N_HEADS, DH, DH), lambda b: (b, 0, 0, 0)),
        pl.BlockSpec((ns, N_HEADS, DH), lambda b: (b, 0, 0)),
        pl.BlockSpec((ns, SUBLANES, GATE_COLS), lambda b: (b, 0, 0)),
        pl.BlockSpec((ns, N_HEADS, DH, DH), lambda b: (b, 0, 0, 0)),
    )
    return pl.pallas_call(
        functools.partial(_sample_kernel, t_last=t_last),
        out_shape=out_shape, grid=(B // ns,), in_specs=in_specs, out_specs=out_specs,
        scratch_shapes=[pltpu.VMEM((ns, CONV_PAD + tt, 2 * D_A), _F32)], name="sample_mixer",
        compiler_params=pltpu.CompilerParams(dimension_semantics=("arbitrary",), vmem_limit_bytes=VMEM_LIMIT),
    )(p, g, conv0, C0, n0, m0, S0, convw, convb, mnorm, lbl, hnorm)


def _ffn(x, mix, wout, nf, wgate, wup, wdown, nfin):
    N, D = x.shape
    tm = min(N, TM_FFN)
    return pl.pallas_call(
        _ffn_kernel, out_shape=jax.ShapeDtypeStruct((N, D), _F32), grid=(N // tm,),
        in_specs=[pl.BlockSpec((tm, D), lambda i: (i, 0)), pl.BlockSpec((tm, D_MODEL), lambda i: (i, 0)),
                  _resident(wout.shape), _resident(nf.shape), _resident(wgate.shape), _resident(wup.shape),
                  _resident(wdown.shape), _resident(nfin.shape)],
        out_specs=pl.BlockSpec((tm, D), lambda i: (i, 0)), name="ffn",
        compiler_params=pltpu.CompilerParams(dimension_semantics=("arbitrary",), vmem_limit_bytes=VMEM_LIMIT),
    )(x, mix, wout, nf, wgate, wup, wdown, nfin)


def kernel(x_prompt, x_sample, state_conv, state_mlstm_C, state_mlstm_n, state_mlstm_m, state_hgrn_S,
           norm_mix, w_in, b_in, conv_w, conv_b, mlstm_norm, hgrn_lb_logits, hgrn_norm, w_out,
           norm_ffn, w_gate, w_up, w_down, norm_final):
    depth = w_in.shape[0]
    assert depth == 1, "single-layer stack"
    B, T, D = x_prompt.shape
    BS, TS, _ = x_sample.shape
    l = 0
    g0 = 4 * D_A
    n_gates = 2 * N_HEADS

    wt = jnp.swapaxes(w_in[l], 0, 1)
    wa = wt[:g0].astype(_BF16)
    wb = wt[g0 + n_gates:].astype(_BF16)
    wg = jnp.pad(wt[g0:g0 + n_gates], ((0, GATE_COLS - n_gates), (0, 0))).astype(_BF16)
    bmain = jnp.concatenate([b_in[l, :g0], b_in[l, g0 + n_gates:]])[None, :]
    bg = jnp.pad(b_in[l, g0:g0 + n_gates], (0, GATE_COLS - n_gates))[None, :]
    nm = norm_mix[l][None, :]
    convw = conv_w[l]
    convb = conv_b[l][None, :]
    mnorm = mlstm_norm[l][None, :]
    hnorm = hgrn_norm[l][None, :]
    lbl = hgrn_lb_logits.astype(_F32)
    wout = w_out[l].astype(_BF16)
    wgate = w_gate[l].astype(_BF16)
    wup = w_up[l].astype(_BF16)
    wdown = w_down[l].astype(_BF16)
    nf = norm_ffn[l][None, :]
    nfin = norm_final[None, :]

    mixer_w = (nm, wa, wb, bmain, wg, bg, convw, convb, mnorm, lbl, hnorm)
    ffn_w = (wout, nf, wgate, wup, wdown, nfin)
    y_p, conv_p, C_p, n_p, m_p, S_p = _prompt_layer(x_prompt, mixer_w, ffn_w)
    y_prompt = y_p.reshape(B, T, D)

    xs_pad = jnp.pad(x_sample, ((0, 0), (0, TT_SAMPLE - TS), (0, 0)))
    p_s, g_s = _sample_proj(xs_pad.reshape(BS * TT_SAMPLE, D), nm, wa, wb, bmain, wg, bg)
    mix_s, conv_s, C_s, n_s, m_s, S_s = _sample_mixer(
        p_s.reshape(BS, TT_SAMPLE, P_MAIN), g_s.reshape(BS, TT_SAMPLE, GATE_COLS), state_conv[l],
        state_mlstm_C[l], state_mlstm_n[l], state_mlstm_m[l][:, :, None], state_hgrn_S[l],
        convw, convb, mnorm, lbl, hnorm, TS - 1)
    y_sample = _ffn(x_sample.reshape(BS * TS, D), mix_s[:, :TS, :].reshape(BS * TS, D_MODEL), wout, nf, wgate,
                    wup, wdown, nfin).reshape(BS, TS, D)

    return (y_prompt, y_sample,
            conv_p[None], C_p[None], n_p[None], m_p[None, :, :N_HEADS, 0], S_p[None],
            conv_s[None], C_s[None], n_s[None], m_s[None, :, :N_HEADS, 0], S_s[None])
```

```python
import functools

import jax
import jax.numpy as jnp
from jax import lax
from jax.experimental import pallas as pl
from jax.experimental.pallas import tpu as pltpu

D_MODEL = 1024
D_A = 512
D_B = 512
N_HEADS = 4
DH = 128
CONV_W = 4
D_FF = 2816
EPS = 1e-6
P_MAIN = 2 * D_A + D_A + D_A + 4 * D_B
GATE_COLS = 128
SUBLANES = 8
CONV_PAD = 8
VMEM_LIMIT = 56 * 1024 * 1024

TT_PROMPT = 256
CHUNK_PROMPT = 256
TT_SAMPLE = 8
TM_FFN = 512
SEQS_PER_STEP = 16

_F32 = jnp.float32
_BF16 = jnp.bfloat16


def _dot_dims(a, b, ca, cb):
    batch = ((0,), (0,)) if a.ndim == 3 and b.ndim == 3 else ((), ())
    return lax.dot_general(a, b, (((a.ndim + ca,), (b.ndim + cb,)), batch), preferred_element_type=_F32)


def _dot(a, b):
    return _dot_dims(a, b, -1, -2)


def _dot_nt(a, b):
    return _dot_dims(a, b, -1, -1)


def _dot_tn(a, b):
    return _dot_dims(a, b, -2, -2)


def _sigmoid(x):
    return 1.0 / (1.0 + jnp.exp(-x))


def _rmsnorm(x, g):
    return x * lax.rsqrt(jnp.mean(x * x, axis=-1, keepdims=True) + EPS) * g


def _bf16_pieces(x):
    hi = x.astype(_BF16)
    r1 = x - hi.astype(_F32)
    mid = r1.astype(_BF16)
    lo = (r1 - mid.astype(_F32)).astype(_BF16)
    return hi, mid, lo


def _tril_matmul_cumsum(x, tril_bf16):
    hi, mid, lo = _bf16_pieces(x)
    return _dot(tril_bf16, hi) + _dot(tril_bf16, mid) + _dot(tril_bf16, lo)


def _triu_matmul_cumsum(x, triu_bf16):
    hi, mid, lo = _bf16_pieces(x)
    return _dot(hi, triu_bf16) + _dot(mid, triu_bf16) + _dot(lo, triu_bf16)


def _log_sigmoid(x):
    return jnp.minimum(x, 0.0) - jnp.log(1.0 + jnp.exp(-jnp.abs(x)))


def _sublane_cumsum(x, row):
    for sh in (1, 2, 4):
        x = x + jnp.where(row >= sh, pltpu.roll(x, sh, x.ndim - 2), 0.0)
    return x


def _row_at(x, row, t):
    return jnp.sum(jnp.where(row == t, x, 0.0), axis=-2, keepdims=True)


def _gate_vectors_small(igc, lfc, tril, eye):
    lfr = jnp.sum(jnp.where(eye, lfc, 0.0), axis=-2, keepdims=True)
    Fc = jnp.sum(jnp.where(tril, lfr, 0.0), axis=-1, keepdims=True)
    uc = igc - Fc
    ur = jnp.sum(jnp.where(eye, uc, 0.0), axis=-2, keepdims=True)
    return Fc, uc, ur


def _mlstm_head(q, k, v, Fc, uc, ur, C0, n0, m0, tril, row, t_last, fill):
    a = Fc + m0
    D = jnp.where(tril, Fc + ur, -jnp.inf)
    mt = jnp.maximum(a, jnp.max(D, axis=-1, keepdims=True))
    w_inter = jnp.exp(a - mt)
    W = jnp.exp(D - mt)
    fill()
    qb = q.astype(_BF16)
    kb = k.astype(_BF16)
    vb = v.astype(_BF16)
    s = _dot_nt(qb, kb) * W
    fill()
    num = w_inter * _dot(qb, C0.astype(_BF16)) + _dot(s.astype(_BF16), vb)
    den = w_inter * jnp.sum(q * n0, axis=-1, keepdims=True) + jnp.sum(s, axis=-1, keepdims=True)
    h = num / jnp.maximum(jnp.abs(den), jnp.exp(-mt))
    fill()
    F_last = _row_at(Fc, row, t_last)
    m_last = _row_at(mt, row, t_last)
    decay = _row_at(w_inter, row, t_last)
    valid = row <= t_last
    wl = jnp.where(valid, jnp.exp(jnp.where(valid, F_last + uc - m_last, 0.0)), 0.0)
    kw = wl * k
    C_new = decay * C0 + _dot_tn(kw.astype(_BF16), vb)
    n_new = decay * n0 + jnp.sum(kw, axis=-2, keepdims=True)
    return h, C_new, n_new, m_last


def _hgrn_core(q, kk, v, f, b, S0, eye_dh, xor_rc, row, t_last, b_ref, qt_ref, kt_ref, nd_ref, fill):
    n_batch, tt, _ = q.shape
    vb = v.astype(_BF16)
    o = _dot((q * jnp.exp(b)).astype(_BF16), S0.astype(_BF16))
    fill()

    if nd_ref is None:
        fz = jnp.where((row & (SUBLANES - 1)) == 0, 0.0, f)
        o = o + jnp.sum(q * kk, axis=-1, keepdims=True) * v
        E = fz
        for d in range(1, SUBLANES):
            if d > 1:
                E = E * pltpu.roll(fz, d - 1, 1)
            A = jnp.sum(q * pltpu.roll(kk, d, 1) * E, axis=-1, keepdims=True)
            o = o + A * pltpu.roll(v, d, 1)
    else:
        for idx, val in enumerate((q, kk, v, f)):
            nd_ref[idx] = val
        group = SUBLANES * SUBLANES
        for n in range(n_batch):
            for g0 in range(0, tt, group):
                views = [[nd_ref[idx, n, pl.ds(g0 + pos, SUBLANES, stride=SUBLANES), :]
                          for pos in range(SUBLANES)] for idx in range(4)]
                qv, kv, vv, fv = views
                for tp in range(SUBLANES):
                    acc = jnp.sum(qv[tp] * kv[tp], axis=-1, keepdims=True) * vv[tp]
                    E = None
                    for sp in range(tp - 1, -1, -1):
                        E = fv[sp + 1] if E is None else E * fv[sp + 1]
                        A = jnp.sum(qv[tp] * kv[sp] * E, axis=-1, keepdims=True)
                        acc = acc + A * vv[sp]
                    nd_ref[4, n, pl.ds(g0 + tp, SUBLANES, stride=SUBLANES), :] = acc
            fill()
        o = o + nd_ref[4]

    if tt > SUBLANES:
        b_ref[...] = b
        A_far = None
        h = SUBLANES
        while h < tt:
            for j in range(tt // h):
                rows = slice(j * h, (j + 1) * h)
                if j % 2 == 1:
                    b_mid = b_ref[:, pl.ds(j * h - 1, 1), :]
                    qt_ref[:, rows, :] = q[:, rows, :] * jnp.exp(b[:, rows, :] - b_mid)
                    kt_ref[:, rows, :] = jnp.zeros((n_batch, h, DH), _F32)
                else:
                    b_mid = b_ref[:, pl.ds((j + 1) * h - 1, 1), :]
                    kt_ref[:, rows, :] = kk[:, rows, :] * jnp.exp(b_mid - b[:, rows, :])
                    qt_ref[:, rows, :] = jnp.zeros((n_batch, h, DH), _F32)
            A_h = _dot_nt(qt_ref[...].astype(_BF16), kt_ref[...].astype(_BF16))
            if 2 * h < tt:
                A_h = jnp.where(xor_rc < 2 * h, A_h, 0.0)
            A_far = A_h if A_far is None else A_far + A_h
            fill()
            h *= 2
        o = o + _dot(A_far.astype(_BF16), vb)

    bL = _row_at(b, row, t_last)
    valid = row <= t_last
    ks = jnp.where(valid, kk * jnp.exp(jnp.where(valid, bL - b, 0.0)), 0.0)
    bL_col = jnp.sum(jnp.where(eye_dh, bL, 0.0), axis=-1, keepdims=True)
    S_new = jnp.exp(bL_col) * S0 + _dot_tn(ks.astype(_BF16), vb)
    return o, S_new


def _headnorm(h, g):
    return h * lax.rsqrt(jnp.mean(h * h, axis=-1, keepdims=True) + EPS) * g


def _mixers(p_ref, g, convw_ref, convb_ref, mnorm_ref, lbl_ref, hnorm_ref,
            mix_ref, convout_ref, C_in, n_in, m_in, S_in, C_out, n_out, m_out, S_out,
            convbuf, b_ref, qt_ref, kt_ref, nd_ref, tt, t_last, fill=lambda: None):
    batched = len(p_ref.shape) == 3
    n_tiles = p_ref.shape[0] if batched else 1
    lead = (slice(None),) if batched else ()

    def rows(start, size):
        return lead + (pl.ds(start, size), slice(None))

    def cols(start, size):
        return lead + (slice(None), slice(start, start + size))

    def head(hd):
        return lead + (hd,)

    def head_row(hd):
        return lead + (slice(hd, hd + 1), slice(None))

    def cat_heads(get):
        parts = [get(hd) for hd in range(N_HEADS)]
        return jnp.concatenate([x if batched else x[None] for x in parts], axis=0)

    def head_cols(x):
        return cat_heads(lambda hd: x[..., hd * DH:(hd + 1) * DH])

    def head_gain(ref):
        return cat_heads(lambda hd: jnp.broadcast_to(ref[:, hd * DH:(hd + 1) * DH], (n_tiles, 1, DH))
                         if batched else ref[:, hd * DH:(hd + 1) * DH])

    def of_head(x, hd):
        return x[hd * n_tiles:(hd + 1) * n_tiles] if batched else x[hd]

    row = lax.broadcasted_iota(jnp.int32, (tt, 1), 0)
    ri = lax.broadcasted_iota(jnp.int32, (tt, tt), 0)
    ci = lax.broadcasted_iota(jnp.int32, (tt, tt), 1)
    tril = ci <= ri
    eye = ci == ri
    xor_rc = ri ^ ci
    rd = lax.broadcasted_iota(jnp.int32, (DH, DH), 0)
    cd = lax.broadcasted_iota(jnp.int32, (DH, DH), 1)
    eye_dh = rd == cd

    prev = CONV_PAD - (CONV_W - 1)
    last = CONV_PAD + t_last - (CONV_W - 2)
    convbuf[rows(CONV_PAD, tt)] = p_ref[cols(0, 2 * D_A)]
    conv = convb_ref[...]
    for j in range(CONV_W):
        conv = conv + convw_ref[j:j + 1, :] * convbuf[rows(prev + j, tt)]
    new_rows = convbuf[rows(last, CONV_W - 1)]
    convout_ref[...] = new_rows
    convbuf[rows(prev, CONV_W - 1)] = new_rows
    fill()
    qk = conv * _sigmoid(conv)
    fill()
    q = head_cols(qk[..., 0:D_A]) * (DH ** -0.5)
    k = head_cols(qk[..., D_A:2 * D_A])

    base = 4 * D_A
    f_raw = p_ref[cols(base + D_B, D_B)]
    logits = lbl_ref[...]
    ex = jnp.exp(logits - jnp.max(logits, axis=0, keepdims=True))
    lb = ex[0:1, :] / jnp.sum(ex, axis=0, keepdims=True)
    f = lb + (1.0 - lb) * _sigmoid(f_raw)
    lf = jnp.log(f)

    lf_all = _log_sigmoid(g)
    if batched or tt == SUBLANES:
        b = _sublane_cumsum(lf, row)
        Fc, uc, ur = _gate_vectors_small(cat_heads(lambda hd: g[..., hd:hd + 1]),
                                         cat_heads(lambda hd: lf_all[..., N_HEADS + hd:N_HEADS + hd + 1]),
                                         tril, eye)
    else:
        tril_bf16 = jnp.where(tril, 1.0, 0.0).astype(_BF16)
        triu_bf16 = jnp.where(ci >= ri, 1.0, 0.0).astype(_BF16)
        cum = _tril_matmul_cumsum(jnp.concatenate([lf, lf_all], axis=1), tril_bf16)
        b = cum[:, 0:D_B]
        g_t = g.T
        F_rows = _triu_matmul_cumsum(_log_sigmoid(g_t[0:2 * N_HEADS, :]), triu_bf16)
        Fc = cat_heads(lambda hd: cum[:, D_B + N_HEADS + hd:D_B + N_HEADS + hd + 1])
        uc = cat_heads(lambda hd: g[:, hd:hd + 1]) - Fc
        ur = cat_heads(lambda hd: g_t[hd:hd + 1, :] - F_rows[N_HEADS + hd:N_HEADS + hd + 1, :])

    fill()
    v = head_cols(p_ref[cols(2 * D_A, D_A)])
    o_raw = head_cols(p_ref[cols(3 * D_A, D_A)])
    C0 = cat_heads(lambda hd: C_in[head(hd)])
    n0 = cat_heads(lambda hd: n_in[head_row(hd)])
    m0 = cat_heads(m_in)
    h, C_new, n_new, m_new = _mlstm_head(q, k, v, Fc, uc, ur, C0, n0, m0, tril, row, t_last, fill)
    out_a = _sigmoid(o_raw) * _headnorm(h, head_gain(mnorm_ref))
    m_new = jnp.broadcast_to(m_new, m_new.shape[:-1] + (GATE_COLS,))
    for hd in range(N_HEADS):
        C_out[head(hd)] = of_head(C_new, hd)
        n_out[head_row(hd)] = of_head(n_new, hd)
        m_out[head_row(hd)] = of_head(m_new, hd)
        mix_ref[cols(hd * DH, DH)] = of_head(out_a, hd).astype(mix_ref.dtype)

    q_raw = p_ref[cols(base, D_B)]
    g_raw = p_ref[cols(base + 3 * D_B, D_B)]
    kk = (1.0 - lb) * _sigmoid(-f_raw)
    qs = q_raw * _sigmoid(q_raw)
    S0 = cat_heads(lambda hd: S_in[head(hd)])
    o, S_new = _hgrn_core(head_cols(qs), head_cols(kk), head_cols(p_ref[cols(base + 2 * D_B, D_B)]),
                          head_cols(f), head_cols(b), S0, eye_dh, xor_rc, row, t_last,
                          b_ref, qt_ref, kt_ref, nd_ref, fill)
    out_b = _headnorm(o, head_gain(hnorm_ref)) * head_cols(g_raw * _sigmoid(g_raw))
    for hd in range(N_HEADS):
        S_out[head(hd)] = of_head(S_new, hd)
        mix_ref[cols(D_A + hd * DH, DH)] = of_head(out_b, hd).astype(mix_ref.dtype)


def _project(x, nm_ref, wa_ref, wb_ref, bmain_ref, wg_ref, bg_ref):
    h = _rmsnorm(x, nm_ref[...]).astype(_BF16)
    half = P_MAIN // 2
    p = jnp.concatenate([_dot_nt(h, wa_ref[...]) + bmain_ref[:, 0:half],
                         _dot_nt(h, wb_ref[...]) + bmain_ref[:, half:P_MAIN]], axis=1)
    g = _dot_nt(h, wg_ref[...]) + bg_ref[...]
    return p, g


def _layer_kernel(x_ref, xprev_ref, nm_ref, wa_ref, wb_ref, bmain_ref, wg_ref, bg_ref, convw_ref, convb_ref,
                  mnorm_ref, lbl_ref, hnorm_ref, wout_ref, nf_ref, wgate_ref, wup_ref, wdown_ref, nfin_ref,
                  y_ref, convp_ref, C_ref, n_ref, m_ref, S_ref,
                  p_scr, g_scr, mix_scr, convbuf, b_ref, qt_ref, kt_ref, nd_ref, n_tiles, tiles_per_row):
    tt = x_ref.shape[0]
    t = pl.program_id(0)
    ff_edges = (0, 768, 1536, 2176, D_FF)
    fill_at = (0, 1, 2, 8, 10, 12, 14, 16, 18, 20)

    def ffn_pieces():
        held = {}

        def out_proj():
            x1 = xprev_ref[...] + _dot(mix_scr[...], wout_ref[...])
            held.update(x1=x1, h2=_rmsnorm(x1, nf_ref[...]).astype(_BF16))

        def gate(c):
            cs = pl.ds(ff_edges[c], ff_edges[c + 1] - ff_edges[c])

            def emit():
                held["gate"] = _dot(held["h2"], wgate_ref[:, cs])
            return emit

        def up_down(c):
            cs = pl.ds(ff_edges[c], ff_edges[c + 1] - ff_edges[c])

            def emit():
                gt = held["gate"]
                ff = (gt * _sigmoid(gt) * _dot(held["h2"], wup_ref[:, cs])).astype(_BF16)
                part = _dot(ff, wdown_ref[cs, :])
                held["x2"] = (held["x1"] if c == 0 else held["x2"]) + part
            return emit

        def final():
            y_ref[...] = _rmsnorm(held["x2"], nfin_ref[...])

        pieces = [out_proj]
        for c in range(len(ff_edges) - 1):
            pieces += [gate(c), up_down(c)]
        return pieces + [final]

    @pl.when(t == 0)
    def _():
        mix_scr[...] = jnp.zeros_like(mix_scr)

    @pl.when(jnp.logical_and(t < n_tiles, t % tiles_per_row == 0))
    def _():
        C_ref[...] = jnp.zeros_like(C_ref)
        n_ref[...] = jnp.zeros_like(n_ref)
        m_ref[...] = jnp.zeros_like(m_ref)
        S_ref[...] = jnp.zeros_like(S_ref)
        convbuf[0:CONV_PAD, :] = jnp.zeros((CONV_PAD, 2 * D_A), _F32)

    @pl.when(t < n_tiles)
    def _():
        pieces = ffn_pieces()
        calls = [0]

        def fill():
            if pieces and calls[0] in fill_at:
                pieces.pop(0)()
            calls[0] += 1

        fill()
        p, g = _project(x_ref[...], nm_ref, wa_ref, wb_ref, bmain_ref, wg_ref, bg_ref)
        p_scr[...] = p
        g_scr[...] = g
        ck = CHUNK_PROMPT
        for r0 in range(0, tt, ck):
            _mixers(p_scr.at[pl.ds(r0, ck)], g_scr[pl.ds(r0, ck), :], convw_ref, convb_ref, mnorm_ref, lbl_ref,
                    hnorm_ref, mix_scr.at[pl.ds(r0, ck)], convp_ref, C_ref, n_ref,
                    lambda hd: m_ref[hd:hd + 1, 0:1], S_ref, C_ref, n_ref, m_ref, S_ref,
                    convbuf, b_ref, qt_ref, kt_ref, nd_ref, ck, ck - 1, fill)
        while pieces:
            pieces.pop(0)()

    @pl.when(t == n_tiles)
    def _():
        for piece in ffn_pieces():
            piece()


def _proj_kernel(x_ref, nm_ref, wa_ref, wb_ref, bmain_ref, wg_ref, bg_ref, p_ref, g_ref):
    p, g = _project(x_ref[...], nm_ref, wa_ref, wb_ref, bmain_ref, wg_ref, bg_ref)
    p_ref[...] = p
    g_ref[...] = g


def _sample_kernel(p_ref, g_ref, conv0_ref, C0_ref, n0_ref, m0_ref, S0_ref,
                   convw_ref, convb_ref, mnorm_ref, lbl_ref, hnorm_ref,
                   mix_ref, convs_ref, C_ref, n_ref, m_ref, S_ref,
                   convbuf, t_last):
    n_seq, tt, _ = p_ref.shape
    lead_rows = CONV_PAD - (CONV_W - 1)
    convbuf[:, 0:lead_rows, :] = jnp.zeros((n_seq, lead_rows, 2 * D_A), _F32)
    convbuf[:, lead_rows:CONV_PAD, :] = conv0_ref[...]
    m_ref[...] = jnp.zeros_like(m_ref)
    _mixers(p_ref, g_ref[...], convw_ref, convb_ref, mnorm_ref, lbl_ref, hnorm_ref,
            mix_ref, convs_ref, C0_ref, n0_ref, lambda hd: m0_ref[:, hd:hd + 1, :], S0_ref,
            C_ref, n_ref, m_ref, S_ref, convbuf, None, None, None, None, tt, t_last)


def _ffn_kernel(x_ref, mix_ref, wout_ref, nf_ref, wgate_ref, wup_ref, wdown_ref, nfin_ref, y_ref):
    x1 = x_ref[...] + _dot(mix_ref[...], wout_ref[...])
    h2 = _rmsnorm(x1, nf_ref[...]).astype(_BF16)
    gate = _dot(h2, wgate_ref[...])
    up = _dot(h2, wup_ref[...])
    ff = (gate * _sigmoid(gate) * up).astype(_BF16)
    x2 = x1 + _dot(ff, wdown_ref[...])
    y_ref[...] = _rmsnorm(x2, nfin_ref[...])


def _resident(shape):
    nd = len(shape)
    return pl.BlockSpec(shape, lambda *_: (0,) * nd, pipeline_mode=pl.Buffered(1))


def _prompt_layer(x, mixer_w, ffn_w):
    B, T, D = x.shape
    tt = TT_PROMPT
    ck = CHUNK_PROMPT
    n_tiles = B * T // tt
    tiles_per_row = T // tt
    x2 = x.reshape(B * T, D)
    weights = tuple(mixer_w) + tuple(ffn_w)

    def cur(t):
        return jnp.minimum(t, n_tiles - 1)

    def prev(t):
        return jnp.maximum(t - 1, 0)

    def row(t):
        return cur(t) // tiles_per_row

    out_shape = (
        jax.ShapeDtypeStruct((B * T, D), _F32),
        jax.ShapeDtypeStruct((B, CONV_W - 1, 2 * D_A), _F32),
        jax.ShapeDtypeStruct((B, N_HEADS, DH, DH), _F32),
        jax.ShapeDtypeStruct((B, N_HEADS, DH), _F32),
        jax.ShapeDtypeStruct((B, SUBLANES, GATE_COLS), _F32),
        jax.ShapeDtypeStruct((B, N_HEADS, DH, DH), _F32),
    )
    in_specs = [
        pl.BlockSpec((tt, D), lambda t: (cur(t), 0)),
        pl.BlockSpec((tt, D), lambda t: (prev(t), 0)),
    ] + [_resident(w.shape) for w in weights]
    out_specs = (
        pl.BlockSpec((tt, D), lambda t: (prev(t), 0)),
        pl.BlockSpec((None, CONV_W - 1, 2 * D_A), lambda t: (row(t), 0, 0)),
        pl.BlockSpec((None, N_HEADS, DH, DH), lambda t: (row(t), 0, 0, 0)),
        pl.BlockSpec((None, N_HEADS, DH), lambda t: (row(t), 0, 0)),
        pl.BlockSpec((None, SUBLANES, GATE_COLS), lambda t: (row(t), 0, 0)),
        pl.BlockSpec((None, N_HEADS, DH, DH), lambda t: (row(t), 0, 0, 0)),
    )
    scratch = [
        pltpu.VMEM((tt, P_MAIN), _F32), pltpu.VMEM((tt, GATE_COLS), _F32),
        pltpu.VMEM((tt, D_MODEL), _BF16),
        pltpu.VMEM((CONV_PAD + ck, 2 * D_A), _F32),
        pltpu.VMEM((N_HEADS, ck, DH), _F32),
        pltpu.VMEM((N_HEADS, ck, DH), _F32),
        pltpu.VMEM((N_HEADS, ck, DH), _F32),
        pltpu.VMEM((5, N_HEADS, ck, DH), _F32),
    ]
    return pl.pallas_call(
        functools.partial(_layer_kernel, n_tiles=n_tiles, tiles_per_row=tiles_per_row),
        out_shape=out_shape, grid=(n_tiles + 1,), in_specs=in_specs, out_specs=out_specs,
        scratch_shapes=scratch, name="prompt_layer",
        compiler_params=pltpu.CompilerParams(dimension_semantics=("arbitrary",), vmem_limit_bytes=VMEM_LIMIT),
    )(x2, x2, *weights)


def _sample_proj(x, nm, wa, wb, bmain, wg, bg):
    N, D = x.shape
    tm = min(N, 512)
    return pl.pallas_call(
        _proj_kernel,
        out_shape=(jax.ShapeDtypeStruct((N, P_MAIN), _F32), jax.ShapeDtypeStruct((N, GATE_COLS), _F32)),
        grid=(N // tm,),
        in_specs=[pl.BlockSpec((tm, D), lambda i: (i, 0))] + [_resident(a.shape) for a in (nm, wa, wb, bmain, wg, bg)],
        out_specs=(pl.BlockSpec((tm, P_MAIN), lambda i: (i, 0)), pl.BlockSpec((tm, GATE_COLS), lambda i: (i, 0))),
        name="sample_proj",
        compiler_params=pltpu.CompilerParams(dimension_semantics=("arbitrary",), vmem_limit_bytes=VMEM_LIMIT),
    )(x, nm, wa, wb, bmain, wg, bg)


def _sample_mixer(p, g, conv0, C0, n0, m0, S0, convw, convb, mnorm, lbl, hnorm, t_last):
    B, tt, _ = p.shape
    out_shape = (
        jax.ShapeDtypeStruct((B, tt, D_MODEL), _BF16),
        jax.ShapeDtypeStruct((B, CONV_W - 1, 2 * D_A), _F32),
        jax.ShapeDtypeStruct((B, N_HEADS, DH, DH), _F32),
        jax.ShapeDtypeStruct((B, N_HEADS, DH), _F32),
        jax.ShapeDtypeStruct((B, SUBLANES, GATE_COLS), _F32),
        jax.ShapeDtypeStruct((B, N_HEADS, DH, DH), _F32),
    )
    ns = SEQS_PER_STEP
    in_specs = [
        pl.BlockSpec((ns, tt, P_MAIN), lambda b: (b, 0, 0)),
        pl.BlockSpec((ns, tt, GATE_COLS), lambda b: (b, 0, 0)),
        pl.BlockSpec((ns, CONV_W - 1, 2 * D_A), lambda b: (b, 0, 0)),
        pl.BlockSpec((ns, N_HEADS, DH, DH), lambda b: (b, 0, 0, 0)),
        pl.BlockSpec((ns, N_HEADS, DH), lambda b: (b, 0, 0)),
        pl.BlockSpec((ns, N_HEADS, 1), lambda b: (b, 0, 0)),
        pl.BlockSpec((ns, N_HEADS, DH, DH), lambda b: (b, 0, 0, 0)),
        _resident(convw.shape), _resident(convb.shape), _resident(mnorm.shape), _resident(lbl.shape),
        _resident(hnorm.shape),
    ]
    out_specs = (
        pl.BlockSpec((ns, tt, D_MODEL), lambda b: (b, 0, 0)),
        pl.BlockSpec((ns, CONV_W - 1, 2 * D_A), lambda b: (b, 0, 0)),
        pl.BlockSpec((ns, N_HEADS, DH, DH), lambda b: (b, 0, 0, 0)),
        pl.BlockSpec((ns, N_HEADS, DH), lambda b: (b, 0, 0)),
        pl.BlockSpec((ns, SUBLANES, GATE_COLS), lambda b: (b, 0, 0)),
        pl.BlockSpec((ns, N_HEADS, DH, DH), lambda b: (b, 0, 0, 0)),
    )
    return pl.pallas_call(
        functools.partial(_sample_kernel, t_last=t_last),
        out_shape=out_shape, grid=(B // ns,), in_specs=in_specs, out_specs=out_specs,
        scratch_shapes=[pltpu.VMEM((ns, CONV_PAD + tt, 2 * D_A), _F32)], name="sample_mixer",
        compiler_params=pltpu.CompilerParams(dimension_semantics=("arbitrary",), vmem_limit_bytes=VMEM_LIMIT),
    )(p, g, conv0, C0, n0, m0, S0, convw, convb, mnorm, lbl, hnorm)


def _ffn(x, mix, wout, nf, wgate, wup, wdown, nfin):
    N, D = x.shape
    tm = min(N, TM_FFN)
    return pl.pallas_call(
        _ffn_kernel, out_shape=jax.ShapeDtypeStruct((N, D), _F32), grid=(N // tm,),
        in_specs=[pl.BlockSpec((tm, D), lambda i: (i, 0)), pl.BlockSpec((tm, D_MODEL), lambda i: (i, 0)),
                  _resident(wout.shape), _resident(nf.shape), _resident(wgate.shape), _resident(wup.shape),
                  _resident(wdown.shape), _resident(nfin.shape)],
        out_specs=pl.BlockSpec((tm, D), lambda i: (i, 0)), name="ffn",
        compiler_params=pltpu.CompilerParams(dimension_semantics=("arbitrary",), vmem_limit_bytes=VMEM_LIMIT),
    )(x, mix, wout, nf, wgate, wup, wdown, nfin)


def kernel(x_prompt, x_sample, state_conv, state_mlstm_C, state_mlstm_n, state_mlstm_m, state_hgrn_S,
           norm_mix, w_in, b_in, conv_w, conv_b, mlstm_norm, hgrn_lb_logits, hgrn_norm, w_out,
           norm_ffn, w_gate, w_up, w_down, norm_final):
    depth = w_in.shape[0]
    assert depth == 1, "single-layer stack"
    B, T, D = x_prompt.shape
    BS, TS, _ = x_sample.shape
    l = 0
    g0 = 4 * D_A
    n_gates = 2 * N_HEADS

    wt = jnp.swapaxes(w_in[l], 0, 1)
    wa = wt[:g0].astype(_BF16)
    wb = wt[g0 + n_gates:].astype(_BF16)
    wg = jnp.pad(wt[g0:g0 + n_gates], ((0, GATE_COLS - n_gates), (0, 0))).astype(_BF16)
    bmain = jnp.concatenate([b_in[l, :g0], b_in[l, g0 + n_gates:]])[None, :]
    bg = jnp.pad(b_in[l, g0:g0 + n_gates], (0, GATE_COLS - n_gates))[None, :]
    nm = norm_mix[l][None, :]
    convw = conv_w[l]
    convb = conv_b[l][None, :]
    mnorm = mlstm_norm[l][None, :]
    hnorm = hgrn_norm[l][None, :]
    lbl = hgrn_lb_logits.astype(_F32)
    wout = w_out[l].astype(_BF16)
    wgate = w_gate[l].astype(_BF16)
    wup = w_up[l].astype(_BF16)
    wdown = w_down[l].astype(_BF16)
    nf = norm_ffn[l][None, :]
    nfin = norm_final[None, :]

    mixer_w = (nm, wa, wb, bmain, wg, bg, convw, convb, mnorm, lbl, hnorm)
    ffn_w = (wout, nf, wgate, wup, wdown, nfin)
    y_p, conv_p, C_p, n_p, m_p, S_p = _prompt_layer(x_prompt, mixer_w, ffn_w)
    y_prompt = y_p.reshape(B, T, D)

    xs_pad = jnp.pad(x_sample, ((0, 0), (0, TT_SAMPLE - TS), (0, 0)))
    p_s, g_s = _sample_proj(xs_pad.reshape(BS * TT_SAMPLE, D), nm, wa, wb, bmain, wg, bg)
    mix_s, conv_s, C_s, n_s, m_s, S_s = _sample_mixer(
        p_s.reshape(BS, TT_SAMPLE, P_MAIN), g_s.reshape(BS, TT_SAMPLE, GATE_COLS), state_conv[l],
        state_mlstm_C[l], state_mlstm_n[l], state_mlstm_m[l][:, :, None], state_hgrn_S[l],
        convw, convb, mnorm, lbl, hnorm, TS - 1)
    y_sample = _ffn(x_sample.reshape(BS * TS, D), mix_s[:, :TS, :].reshape(BS * TS, D_MODEL), wout, nf, wgate,
                    wup, wdown, nfin).reshape(BS, TS, D)

    return (y_prompt, y_sample,
            conv_p[None], C_p[None], n_p[None], m_p[None, :, :N_HEADS, 0], S_p[None],
            conv_s[None], C_s[None], n_s[None], m_s[None, :, :N_HEADS, 0], S_s[None])
```

```python
import functools

import jax
import jax.numpy as jnp
from jax import lax
from jax.experimental import pallas as pl
from jax.experimental.pallas import tpu as pltpu

D_MODEL = 1024
D_A = 512
D_B = 512
N_HEADS = 4
DH = 128
CONV_W = 4
D_FF = 2816
EPS = 1e-6
P_MAIN = 2 * D_A + D_A + D_A + 4 * D_B
GATE_COLS = 128
SUBLANES = 8
CONV_PAD = 8
VMEM_LIMIT = 56 * 1024 * 1024

TT_PROMPT = 256
CHUNK_PROMPT = 256
TT_SAMPLE = 8
TM_FFN = 512
SEQS_PER_STEP = 16

_F32 = jnp.float32
_BF16 = jnp.bfloat16


def _dot_dims(a, b, ca, cb):
    batch = ((0,), (0,)) if a.ndim == 3 and b.ndim == 3 else ((), ())
    return lax.dot_general(a, b, (((a.ndim + ca,), (b.ndim + cb,)), batch), preferred_element_type=_F32)


def _dot(a, b):
    return _dot_dims(a, b, -1, -2)


def _dot_nt(a, b):
    return _dot_dims(a, b, -1, -1)


def _dot_tn(a, b):
    return _dot_dims(a, b, -2, -2)


def _sigmoid(x):
    return 1.0 / (1.0 + jnp.exp(-x))


def _rmsnorm(x, g):
    return x * lax.rsqrt(jnp.mean(x * x, axis=-1, keepdims=True) + EPS) * g


def _bf16_pieces(x):
    hi = x.astype(_BF16)
    r1 = x - hi.astype(_F32)
    mid = r1.astype(_BF16)
    lo = (r1 - mid.astype(_F32)).astype(_BF16)
    return hi, mid, lo


def _tril_matmul_cumsum(x, tril_bf16):
    hi, mid, lo = _bf16_pieces(x)
    return _dot(tril_bf16, hi) + _dot(tril_bf16, mid) + _dot(tril_bf16, lo)


def _triu_matmul_cumsum(x, triu_bf16):
    hi, mid, lo = _bf16_pieces(x)
    return _dot(hi, triu_bf16) + _dot(mid, triu_bf16) + _dot(lo, triu_bf16)


def _log_sigmoid(x):
    return jnp.minimum(x, 0.0) - jnp.log(1.0 + jnp.exp(-jnp.abs(x)))


def _sublane_cumsum(x, row):
    for sh in (1, 2, 4):
        x = x + jnp.where(row >= sh, pltpu.roll(x, sh, x.ndim - 2), 0.0)
    return x


def _row_at(x, row, t):
    return jnp.sum(jnp.where(row == t, x, 0.0), axis=-2, keepdims=True)


def _gate_vectors_small(igc, lfc, tril, eye):
    lfr = jnp.sum(jnp.where(eye, lfc, 0.0), axis=-2, keepdims=True)
    Fc = jnp.sum(jnp.where(tril, lfr, 0.0), axis=-1, keepdims=True)
    uc = igc - Fc
    ur = jnp.sum(jnp.where(eye, uc, 0.0), axis=-2, keepdims=True)
    return Fc, uc, ur


def _mlstm_head(q, k, v, Fc, uc, ur, C0, n0, m0, tril, row, t_last, fill):
    a = Fc + m0
    D = jnp.where(tril, Fc + ur, -jnp.inf)
    mt = jnp.maximum(a, jnp.max(D, axis=-1, keepdims=True))
    w_inter = jnp.exp(a - mt)
    W = jnp.exp(D - mt)
    fill()
    qb = q.astype(_BF16)
    kb = k.astype(_BF16)
    vb = v.astype(_BF16)
    s = _dot_nt(qb, kb) * W
    fill()
    num = w_inter * _dot(qb, C0.astype(_BF16)) + _dot(s.astype(_BF16), vb)
    den = w_inter * jnp.sum(q * n0, axis=-1, keepdims=True) + jnp.sum(s, axis=-1, keepdims=True)
    h = num / jnp.maximum(jnp.abs(den), jnp.exp(-mt))
    fill()
    F_last = _row_at(Fc, row, t_last)
    m_last = _row_at(mt, row, t_last)
    decay = _row_at(w_inter, row, t_last)
    valid = row <= t_last
    wl = jnp.where(valid, jnp.exp(jnp.where(valid, F_last + uc - m_last, 0.0)), 0.0)
    kw = wl * k
    C_new = decay * C0 + _dot_tn(kw.astype(_BF16), vb)
    n_new = decay * n0 + jnp.sum(kw, axis=-2, keepdims=True)
    return h, C_new, n_new, m_last


def _hgrn_core(q, kk, v, f, b, S0, eye_dh, xor_rc, row, t_last, b_ref, qt_ref, kt_ref, nd_ref, fill):
    n_batch, tt, _ = q.shape
    vb = v.astype(_BF16)
    o = _dot((q * jnp.exp(b)).astype(_BF16), S0.astype(_BF16))
    fill()

    if nd_ref is None:
        fz = jnp.where((row & (SUBLANES - 1)) == 0, 0.0, f)
        o = o + jnp.sum(q * kk, axis=-1, keepdims=True) * v
        E = fz
        for d in range(1, SUBLANES):
            if d > 1:
                E = E * pltpu.roll(fz, d - 1, 1)
            A = jnp.sum(q * pltpu.roll(kk, d, 1) * E, axis=-1, keepdims=True)
            o = o + A * pltpu.roll(v, d, 1)
    else:
        for idx, val in enumerate((q, kk, v, f)):
            nd_ref[idx] = val
        group = SUBLANES * SUBLANES
        for n in range(n_batch):
            for g0 in range(0, tt, group):
                views = [[nd_ref[idx, n, pl.ds(g0 + pos, SUBLANES, stride=SUBLANES), :]
                          for pos in range(SUBLANES)] for idx in range(4)]
                qv, kv, vv, fv = views
                for tp in range(SUBLANES):
                    acc = jnp.sum(qv[tp] * kv[tp], axis=-1, keepdims=True) * vv[tp]
                    E = None
                    for sp in range(tp - 1, -1, -1):
                        E = fv[sp + 1] if E is None else E * fv[sp + 1]
                        A = jnp.sum(qv[tp] * kv[sp] * E, axis=-1, keepdims=True)
                        acc = acc + A * vv[sp]
                    nd_ref[4, n, pl.ds(g0 + tp, SUBLANES, stride=SUBLANES), :] = acc
            fill()
        o = o + nd_ref[4]

    if tt > SUBLANES:
        b_ref[...] = b
        A_far = None
        h = SUBLANES
        while h < tt:
            for j in range(tt // h):
                rows = slice(j * h, (j + 1) * h)
                if j % 2 == 1:
                    b_mid = b_ref[:, pl.ds(j * h - 1, 1), :]
                    qt_ref[:, rows, :] = q[:, rows, :] * jnp.exp(b[:, rows, :] - b_mid)
                    kt_ref[:, rows, :] = jnp.zeros((n_batch, h, DH), _F32)
                else:
                    b_mid = b_ref[:, pl.ds((j + 1) * h - 1, 1), :]
                    kt_ref[:, rows, :] = kk[:, rows, :] * jnp.exp(b_mid - b[:, rows, :])
                    qt_ref[:, rows, :] = jnp.zeros((n_batch, h, DH), _F32)
            A_h = _dot_nt(qt_ref[...].astype(_BF16), kt_ref[...].astype(_BF16))
            if 2 * h < tt:
                A_h = jnp.where(xor_rc < 2 * h, A_h, 0.0)
            A_far = A_h if A_far is None else A_far + A_h
            fill()
            h *= 2
        o = o + _dot(A_far.astype(_BF16), vb)

    bL = _row_at(b, row, t_last)
    valid = row <= t_last
    ks = jnp.where(valid, kk * jnp.exp(jnp.where(valid, bL - b, 0.0)), 0.0)
    bL_col = jnp.sum(jnp.where(eye_dh, bL, 0.0), axis=-1, keepdims=True)
    S_new = jnp.exp(bL_col) * S0 + _dot_tn(ks.astype(_BF16), vb)
    return o, S_new


def _headnorm(h, g):
    return h * lax.rsqrt(jnp.mean(h * h, axis=-1, keepdims=True) + EPS) * g


def _mixers(p_ref, g, convw_ref, convb_ref, mnorm_ref, lbl_ref, hnorm_ref,
            mix_ref, convout_ref, C_in, n_in, m_in, S_in, C_out, n_out, m_out, S_out,
            convbuf, b_ref, qt_ref, kt_ref, nd_ref, tt, t_last, fill=lambda: None):
    batched = len(p_ref.shape) == 3
    n_tiles = p_ref.shape[0] if batched else 1
    lead = (slice(None),) if batched else ()

    def rows(start, size):
        return lead + (pl.ds(start, size), slice(None))

    def cols(start, size):
        return lead + (slice(None), slice(start, start + size))

    def head(hd):
        return lead + (hd,)

    def head_row(hd):
        return lead + (slice(hd, hd + 1), slice(None))

    def cat_heads(get):
        parts = [get(hd) for hd in range(N_HEADS)]
        return jnp.concatenate([x if batched else x[None] for x in parts], axis=0)

    def head_cols(x):
        return cat_heads(lambda hd: x[..., hd * DH:(hd + 1) * DH])

    def head_gain(ref):
        return cat_heads(lambda hd: jnp.broadcast_to(ref[:, hd * DH:(hd + 1) * DH], (n_tiles, 1, DH))
                         if batched else ref[:, hd * DH:(hd + 1) * DH])

    def of_head(x, hd):
        return x[hd * n_tiles:(hd + 1) * n_tiles] if batched else x[hd]

    row = lax.broadcasted_iota(jnp.int32, (tt, 1), 0)
    ri = lax.broadcasted_iota(jnp.int32, (tt, tt), 0)
    ci = lax.broadcasted_iota(jnp.int32, (tt, tt), 1)
    tril = ci <= ri
    eye = ci == ri
    xor_rc = ri ^ ci
    rd = lax.broadcasted_iota(jnp.int32, (DH, DH), 0)
    cd = lax.broadcasted_iota(jnp.int32, (DH, DH), 1)
    eye_dh = rd == cd

    prev = CONV_PAD - (CONV_W - 1)
    last = CONV_PAD + t_last - (CONV_W - 2)
    convbuf[rows(CONV_PAD, tt)] = p_ref[cols(0, 2 * D_A)]
    conv = convb_ref[...]
    for j in range(CONV_W):
        conv = conv + convw_ref[j:j + 1, :] * convbuf[rows(prev + j, tt)]
    new_rows = convbuf[rows(last, CONV_W - 1)]
    convout_ref[...] = new_rows
    convbuf[rows(prev, CONV_W - 1)] = new_rows
    fill()
    qk = conv * _sigmoid(conv)
    fill()
    q = head_cols(qk[..., 0:D_A]) * (DH ** -0.5)
    k = head_cols(qk[..., D_A:2 * D_A])

    base = 4 * D_A
    f_raw = p_ref[cols(base + D_B, D_B)]
    logits = lbl_ref[...]
    ex = jnp.exp(logits - jnp.max(logits, axis=0, keepdims=True))
    lb = ex[0:1, :] / jnp.sum(ex, axis=0, keepdims=True)
    f = lb + (1.0 - lb) * _sigmoid(f_raw)
    lf = jnp.log(f)

    lf_all = _log_sigmoid(g)
    if batched or tt == SUBLANES:
        b = _sublane_cumsum(lf, row)
        Fc, uc, ur = _gate_vectors_small(cat_heads(lambda hd: g[..., hd:hd + 1]),
                                         cat_heads(lambda hd: lf_all[..., N_HEADS + hd:N_HEADS + hd + 1]),
                                         tril, eye)
    else:
        tril_bf16 = jnp.where(tril, 1.0, 0.0).astype(_BF16)
        triu_bf16 = jnp.where(ci >= ri, 1.0, 0.0).astype(_BF16)
        cum = _tril_matmul_cumsum(jnp.concatenate([lf, lf_all], axis=1), tril_bf16)
        b = cum[:, 0:D_B]
        g_t = g.T
        F_rows = _triu_matmul_cumsum(_log_sigmoid(g_t[0:2 * N_HEADS, :]), triu_bf16)
        Fc = cat_heads(lambda hd: cum[:, D_B + N_HEADS + hd:D_B + N_HEADS + hd + 1])
        uc = cat_heads(lambda hd: g[:, hd:hd + 1]) - Fc
        ur = cat_heads(lambda hd: g_t[hd:hd + 1, :] - F_rows[N_HEADS + hd:N_HEADS + hd + 1, :])

    fill()
    v = head_cols(p_ref[cols(2 * D_A, D_A)])
    o_raw = head_cols(p_ref[cols(3 * D_A, D_A)])
    C0 = cat_heads(lambda hd: C_in[head(hd)])
    n0 = cat_heads(lambda hd: n_in[head_row(hd)])
    m0 = cat_heads(m_in)
    h, C_new, n_new, m_new = _mlstm_head(q, k, v, Fc, uc, ur, C0, n0, m0, tril, row, t_last, fill)
    out_a = _sigmoid(o_raw) * _headnorm(h, head_gain(mnorm_ref))
    m_new = jnp.broadcast_to(m_new, m_new.shape[:-1] + (GATE_COLS,))
    for hd in range(N_HEADS):
        C_out[head(hd)] = of_head(C_new, hd)
        n_out[head_row(hd)] = of_head(n_new, hd)
        m_out[head_row(hd)] = of_head(m_new, hd)
        mix_ref[cols(hd * DH, DH)] = of_head(out_a, hd).astype(mix_ref.dtype)

    q_raw = p_ref[cols(base, D_B)]
    g_raw = p_ref[cols(base + 3 * D_B, D_B)]
    kk = (1.0 - lb) * _sigmoid(-f_raw)
    qs = q_raw * _sigmoid(q_raw)
    S0 = cat_heads(lambda hd: S_in[head(hd)])
    o, S_new = _hgrn_core(head_cols(qs), head_cols(kk), head_cols(p_ref[cols(base + 2 * D_B, D_B)]),
                          head_cols(f), head_cols(b), S0, eye_dh, xor_rc, row, t_last,
                          b_ref, qt_ref, kt_ref, nd_ref, fill)
    out_b = _headnorm(o, head_gain(hnorm_ref)) * head_cols(g_raw * _sigmoid(g_raw))
    for hd in range(N_HEADS):
        S_out[head(hd)] = of_head(S_new, hd)
        mix_ref[cols(D_A + hd * DH, DH)] = of_head(out_b, hd).astype(mix_ref.dtype)


def _project(x, nm_ref, wa_ref, wb_ref, bmain_ref, wg_ref, bg_ref):
    h = _rmsnorm(x, nm_ref[...]).astype(_BF16)
    half = P_MAIN // 2
    p = jnp.concatenate([_dot_nt(h, wa_ref[...]) + bmain_ref[:, 0:half],
                         _dot_nt(h, wb_ref[...]) + bmain_ref[:, half:P_MAIN]], axis=1)
    g = _dot_nt(h, wg_ref[...]) + bg_ref[...]
    return p, g


def _layer_kernel(x_ref, xprev_ref, nm_ref, wa_ref, wb_ref, bmain_ref, wg_ref, bg_ref, convw_ref, convb_ref,
                  mnorm_ref, lbl_ref, hnorm_ref, wout_ref, nf_ref, wgate_ref, wup_ref, wdown_ref, nfin_ref,
                  y_ref, convp_ref, C_ref, n_ref, m_ref, S_ref,
                  p_scr, g_scr, mix_scr, convbuf, b_ref, qt_ref, kt_ref, nd_ref, n_tiles, tiles_per_row):
    tt = x_ref.shape[0]
    t = pl.program_id(0)
    ff_edges = (0, D_FF)
    fill_at = (0, 3, 6, 9)

    def ffn_pieces():
        held = {}

        def out_proj():
            x1 = xprev_ref[...] + _dot(mix_scr[...], wout_ref[...])
            held.update(x1=x1, h2=_rmsnorm(x1, nf_ref[...]).astype(_BF16))

        def gate(c):
            cs = pl.ds(ff_edges[c], ff_edges[c + 1] - ff_edges[c])

            def emit():
                held["gate"] = _dot(held["h2"], wgate_ref[:, cs])
            return emit

        def up_down(c):
            cs = pl.ds(ff_edges[c], ff_edges[c + 1] - ff_edges[c])

            def emit():
                gt = held["gate"]
                ff = (gt * _sigmoid(gt) * _dot(held["h2"], wup_ref[:, cs])).astype(_BF16)
                part = _dot(ff, wdown_ref[cs, :])
                held["x2"] = (held["x1"] if c == 0 else held["x2"]) + part
            return emit

        def final():
            y_ref[...] = _rmsnorm(held["x2"], nfin_ref[...])

        pieces = [out_proj]
        for c in range(len(ff_edges) - 1):
            pieces += [gate(c), up_down(c)]
        return pieces + [final]

    @pl.when(t == 0)
    def _():
        mix_scr[...] = jnp.zeros_like(mix_scr)

    @pl.when(jnp.logical_and(t < n_tiles, t % tiles_per_row == 0))
    def _():
        C_ref[...] = jnp.zeros_like(C_ref)
        n_ref[...] = jnp.zeros_like(n_ref)
        m_ref[...] = jnp.zeros_like(m_ref)
        S_ref[...] = jnp.zeros_like(S_ref)
        convbuf[0:CONV_PAD, :] = jnp.zeros((CONV_PAD, 2 * D_A), _F32)

    @pl.when(t < n_tiles)
    def _():
        pieces = ffn_pieces()
        calls = [0]

        def fill():
            for _ in range(fill_at.count(calls[0])):
                pieces.pop(0)()
            calls[0] += 1

        fill()
        p, g = _project(x_ref[...], nm_ref, wa_ref, wb_ref, bmain_ref, wg_ref, bg_ref)
        p_scr[...] = p
        g_scr[...] = g
        ck = CHUNK_PROMPT
        for r0 in range(0, tt, ck):
            _mixers(p_scr.at[pl.ds(r0, ck)], g_scr[pl.ds(r0, ck), :], convw_ref, convb_ref, mnorm_ref, lbl_ref,
                    hnorm_ref, mix_scr.at[pl.ds(r0, ck)], convp_ref, C_ref, n_ref,
                    lambda hd: m_ref[hd:hd + 1, 0:1], S_ref, C_ref, n_ref, m_ref, S_ref,
                    convbuf, b_ref, qt_ref, kt_ref, nd_ref, ck, ck - 1, fill)
        while pieces:
            pieces.pop(0)()

    @pl.when(t == n_tiles)
    def _():
        for piece in ffn_pieces():
            piece()


def _proj_kernel(x_ref, nm_ref, wa_ref, wb_ref, bmain_ref, wg_ref, bg_ref, p_ref, g_ref):
    p, g = _project(x_ref[...], nm_ref, wa_ref, wb_ref, bmain_ref, wg_ref, bg_ref)
    p_ref[...] = p
    g_ref[...] = g


def _sample_kernel(p_ref, g_ref, conv0_ref, C0_ref, n0_ref, m0_ref, S0_ref,
                   convw_ref, convb_ref, mnorm_ref, lbl_ref, hnorm_ref,
                   mix_ref, convs_ref, C_ref, n_ref, m_ref, S_ref,
                   convbuf, t_last):
    n_seq, tt, _ = p_ref.shape
    lead_rows = CONV_PAD - (CONV_W - 1)
    convbuf[:, 0:lead_rows, :] = jnp.zeros((n_seq, lead_rows, 2 * D_A), _F32)
    convbuf[:, lead_rows:CONV_PAD, :] = conv0_ref[...]
    m_ref[...] = jnp.zeros_like(m_ref)
    _mixers(p_ref, g_ref[...], convw_ref, convb_ref, mnorm_ref, lbl_ref, hnorm_ref,
            mix_ref, convs_ref, C0_ref, n0_ref, lambda hd: m0_ref[:, hd:hd + 1, :], S0_ref,
            C_ref, n_ref, m_ref, S_ref, convbuf, None, None, None, None, tt, t_last)


def _ffn_kernel(x_ref, mix_ref, wout_ref, nf_ref, wgate_ref, wup_ref, wdown_ref, nfin_ref, y_ref):
    x1 = x_ref[...] + _dot(mix_ref[...], wout_ref[...])
    h2 = _rmsnorm(x1, nf_ref[...]).astype(_BF16)
    gate = _dot(h2, wgate_ref[...])
    up = _dot(h2, wup_ref[...])
    ff = (gate * _sigmoid(gate) * up).astype(_BF16)
    x2 = x1 + _dot(ff, wdown_ref[...])
    y_ref[...] = _rmsnorm(x2, nfin_ref[...])


def _resident(shape):
    nd = len(shape)
    return pl.BlockSpec(shape, lambda *_: (0,) * nd, pipeline_mode=pl.Buffered(1))


def _prompt_layer(x, mixer_w, ffn_w):
    B, T, D = x.shape
    tt = TT_PROMPT
    ck = CHUNK_PROMPT
    n_tiles = B * T // tt
    tiles_per_row = T // tt
    x2 = x.reshape(B * T, D)
    weights = tuple(mixer_w) + tuple(ffn_w)

    def cur(t):
        return jnp.minimum(t, n_tiles - 1)

    def prev(t):
        return jnp.maximum(t - 1, 0)

    def row(t):
        return cur(t) // tiles_per_row

    out_shape = (
        jax.ShapeDtypeStruct((B * T, D), _F32),
        jax.ShapeDtypeStruct((B, CONV_W - 1, 2 * D_A), _F32),
        jax.ShapeDtypeStruct((B, N_HEADS, DH, DH), _F32),
        jax.ShapeDtypeStruct((B, N_HEADS, DH), _F32),
        jax.ShapeDtypeStruct((B, SUBLANES, GATE_COLS), _F32),
        jax.ShapeDtypeStruct((B, N_HEADS, DH, DH), _F32),
    )
    in_specs = [
        pl.BlockSpec((tt, D), lambda t: (cur(t), 0)),
        pl.BlockSpec((tt, D), lambda t: (prev(t), 0)),
    ] + [_resident(w.shape) for w in weights]
    out_specs = (
        pl.BlockSpec((tt, D), lambda t: (prev(t), 0)),
        pl.BlockSpec((None, CONV_W - 1, 2 * D_A), lambda t: (row(t), 0, 0)),
        pl.BlockSpec((None, N_HEADS, DH, DH), lambda t: (row(t), 0, 0, 0)),
        pl.BlockSpec((None, N_HEADS, DH), lambda t: (row(t), 0, 0)),
        pl.BlockSpec((None, SUBLANES, GATE_COLS), lambda t: (row(t), 0, 0)),
        pl.BlockSpec((None, N_HEADS, DH, DH), lambda t: (row(t), 0, 0, 0)),
    )
    scratch = [
        pltpu.VMEM((tt, P_MAIN), _F32), pltpu.VMEM((tt, GATE_COLS), _F32),
        pltpu.VMEM((tt, D_MODEL), _BF16),
        pltpu.VMEM((CONV_PAD + ck, 2 * D_A), _F32),
        pltpu.VMEM((N_HEADS, ck, DH), _F32),
        pltpu.VMEM((N_HEADS, ck, DH), _F32),
        pltpu.VMEM((N_HEADS, ck, DH), _F32),
        pltpu.VMEM((5, N_HEADS, ck, DH), _F32),
    ]
    return pl.pallas_call(
        functools.partial(_layer_kernel, n_tiles=n_tiles, tiles_per_row=tiles_per_row),
        out_shape=out_shape, grid=(n_tiles + 1,), in_specs=in_specs, out_specs=out_specs,
        scratch_shapes=scratch, name="prompt_layer",
        compiler_params=pltpu.CompilerParams(dimension_semantics=("arbitrary",), vmem_limit_bytes=VMEM_LIMIT),
    )(x2, x2, *weights)


def _sample_proj(x, nm, wa, wb, bmain, wg, bg):
    N, D = x.shape
    tm = min(N, 512)
    return pl.pallas_call(
        _proj_kernel,
        out_shape=(jax.ShapeDtypeStruct((N, P_MAIN), _F32), jax.ShapeDtypeStruct((N, GATE_COLS), _F32)),
        grid=(N // tm,),
        in_specs=[pl.BlockSpec((tm, D), lambda i: (i, 0))] + [_resident(a.shape) for a in (nm, wa, wb, bmain, wg, bg)],
        out_specs=(pl.BlockSpec((tm, P_MAIN), lambda i: (i, 0)), pl.BlockSpec((tm, GATE_COLS), lambda i: (i, 0))),
        name="sample_proj",
        compiler_params=pltpu.CompilerParams(dimension_semantics=("arbitrary",), vmem_limit_bytes=VMEM_LIMIT),
    )(x, nm, wa, wb, bmain, wg, bg)


def _sample_mixer(p, g, conv0, C0, n0, m0, S0, convw, convb, mnorm, lbl, hnorm, t_last):
    B, tt, _ = p.shape
    out_shape = (
        jax.ShapeDtypeStruct((B, tt, D_MODEL), _BF16),
        jax.ShapeDtypeStruct((B, CONV_W - 1, 2 * D_A), _F32),
        jax.ShapeDtypeStruct((B, N_HEADS, DH, DH), _F32),
        jax.ShapeDtypeStruct((B, N_HEADS, DH), _F32),
        jax.ShapeDtypeStruct((B, SUBLANES, GATE_COLS), _F32),
        jax.ShapeDtypeStruct((B, N_HEADS, DH, DH), _F32),
    )
    ns = SEQS_PER_STEP
    in_specs = [
        pl.BlockSpec((ns, tt, P_MAIN), lambda b: (b, 0, 0)),
        pl.BlockSpec((ns, tt, GATE_COLS), lambda b: (b, 0, 0)),
        pl.BlockSpec((ns, CONV_W - 1, 2 * D_A), lambda b: (b, 0, 0)),
        pl.BlockSpec((ns, N_HEADS, DH, DH), lambda b: (b, 0, 0, 0)),
        pl.BlockSpec((ns, N_HEADS, DH), lambda b: (b, 0, 0)),
        pl.BlockSpec((ns, N_HEADS, 1), lambda b: (b, 0, 0)),
        pl.BlockSpec((ns, N_HEADS, DH, DH), lambda b: (b, 0, 0, 0)),
        _resident(convw.shape), _resident(convb.shape), _resident(mnorm.shape), _resident(lbl.shape),
        _resident(hnorm.shape),
    ]
    out_specs = (
        pl.BlockSpec((ns, tt, D_MODEL), lambda b: (b, 0, 0)),
        pl.BlockSpec((ns, CONV_W - 1, 2 * D_A), lambda b: (b, 0, 0)),
        pl.BlockSpec((ns, N_HEADS, DH, DH), lambda b: (b, 0, 0, 0)),
        pl.BlockSpec((ns, N_HEADS, DH), lambda b: (b, 0, 0)),
        pl.BlockSpec((ns, SUBLANES, GATE_COLS), lambda b: (b, 0, 0)),
        pl.BlockSpec((ns, N_HEADS, DH, DH), lambda b: (b, 0, 0, 0)),
    )
    return pl.pallas_call(
        functools.partial(_sample_kernel, t_last=t_last),
        out_shape=out_shape, grid=(B // ns,), in_specs=in_specs, out_specs=out_specs,
        scratch_shapes=[pltpu.VMEM((ns, CONV_PAD + tt, 2 * D_A), _F32)], name="sample_mixer",
        compiler_params=pltpu.CompilerParams(dimension_semantics=("arbitrary",), vmem_limit_bytes=VMEM_LIMIT),
    )(p, g, conv0, C0, n0, m0, S0, convw, convb, mnorm, lbl, hnorm)


def _ffn(x, mix, wout, nf, wgate, wup, wdown, nfin):
    N, D = x.shape
    tm = min(N, TM_FFN)
    return pl.pallas_call(
        _ffn_kernel, out_shape=jax.ShapeDtypeStruct((N, D), _F32), grid=(N // tm,),
        in_specs=[pl.BlockSpec((tm, D), lambda i: (i, 0)), pl.BlockSpec((tm, D_MODEL), lambda i: (i, 0)),
                  _resident(wout.shape), _resident(nf.shape), _resident(wgate.shape), _resident(wup.shape),
                  _resident(wdown.shape), _resident(nfin.shape)],
        out_specs=pl.BlockSpec((tm, D), lambda i: (i, 0)), name="ffn",
        compiler_params=pltpu.CompilerParams(dimension_semantics=("arbitrary",), vmem_limit_bytes=VMEM_LIMIT),
    )(x, mix, wout, nf, wgate, wup, wdown, nfin)


def kernel(x_prompt, x_sample, state_conv, state_mlstm_C, state_mlstm_n, state_mlstm_m, state_hgrn_S,
           norm_mix, w_in, b_in, conv_w, conv_b, mlstm_norm, hgrn_lb_logits, hgrn_norm, w_out,
           norm_ffn, w_gate, w_up, w_down, norm_final):
    depth = w_in.shape[0]
    assert depth == 1, "single-layer stack"
    B, T, D = x_prompt.shape
    BS, TS, _ = x_sample.shape
    l = 0
    g0 = 4 * D_A
    n_gates = 2 * N_HEADS

    wt = jnp.swapaxes(w_in[l], 0, 1)
    wa = wt[:g0].astype(_BF16)
    wb = wt[g0 + n_gates:].astype(_BF16)
    wg = jnp.pad(wt[g0:g0 + n_gates], ((0, GATE_COLS - n_gates), (0, 0))).astype(_BF16)
    bmain = jnp.concatenate([b_in[l, :g0], b_in[l, g0 + n_gates:]])[None, :]
    bg = jnp.pad(b_in[l, g0:g0 + n_gates], (0, GATE_COLS - n_gates))[None, :]
    nm = norm_mix[l][None, :]
    convw = conv_w[l]
    convb = conv_b[l][None, :]
    mnorm = mlstm_norm[l][None, :]
    hnorm = hgrn_norm[l][None, :]
    lbl = hgrn_lb_logits.astype(_F32)
    wout = w_out[l].astype(_BF16)
    wgate = w_gate[l].astype(_BF16)
    wup = w_up[l].astype(_BF16)
    wdown = w_down[l].astype(_BF16)
    nf = norm_ffn[l][None, :]
    nfin = norm_final[None, :]

    mixer_w = (nm, wa, wb, bmain, wg, bg, convw, convb, mnorm, lbl, hnorm)
    ffn_w = (wout, nf, wgate, wup, wdown, nfin)
    y_p, conv_p, C_p, n_p, m_p, S_p = _prompt_layer(x_prompt, mixer_w, ffn_w)
    y_prompt = y_p.reshape(B, T, D)

    xs_pad = jnp.pad(x_sample, ((0, 0), (0, TT_SAMPLE - TS), (0, 0)))
    p_s, g_s = _sample_proj(xs_pad.reshape(BS * TT_SAMPLE, D), nm, wa, wb, bmain, wg, bg)
    mix_s, conv_s, C_s, n_s, m_s, S_s = _sample_mixer(
        p_s.reshape(BS, TT_SAMPLE, P_MAIN), g_s.reshape(BS, TT_SAMPLE, GATE_COLS), state_conv[l],
        state_mlstm_C[l], state_mlstm_n[l], state_mlstm_m[l][:, :, None], state_hgrn_S[l],
        convw, convb, mnorm, lbl, hnorm, TS - 1)
    y_sample = _ffn(x_sample.reshape(BS * TS, D), mix_s[:, :TS, :].reshape(BS * TS, D_MODEL), wout, nf, wgate,
                    wup, wdown, nfin).reshape(BS, TS, D)

    return (y_prompt, y_sample,
            conv_p[None], C_p[None], n_p[None], m_p[None, :, :N_HEADS, 0], S_p[None],
            conv_s[None], C_s[None], n_s[None], m_s[None, :, :N_HEADS, 0], S_s[None])
```

```python
import functools

import jax
import jax.numpy as jnp
from jax import lax
from jax.experimental import pallas as pl
from jax.experimental.pallas import tpu as pltpu

D_MODEL = 1024
D_A = 512
D_B = 512
N_HEADS = 4
DH = 128
CONV_W = 4
D_FF = 2816
EPS = 1e-6
P_MAIN = 2 * D_A + D_A + D_A + 4 * D_B
GATE_COLS = 128
SUBLANES = 8
CONV_PAD = 8
VMEM_LIMIT = 56 * 1024 * 1024

TT_PROMPT = 256
CHUNK_PROMPT = 256
TT_SAMPLE = 8
TM_FFN = 512
SEQS_PER_STEP = 16

_F32 = jnp.float32
_BF16 = jnp.bfloat16


def _dot_dims(a, b, ca, cb):
    batch = ((0,), (0,)) if a.ndim == 3 and b.ndim == 3 else ((), ())
    return lax.dot_general(a, b, (((a.ndim + ca,), (b.ndim + cb,)), batch), preferred_element_type=_F32)


def _dot(a, b):
    return _dot_dims(a, b, -1, -2)


def _dot_nt(a, b):
    return _dot_dims(a, b, -1, -1)


def _dot_tn(a, b):
    return _dot_dims(a, b, -2, -2)


def _sigmoid(x):
    return 1.0 / (1.0 + jnp.exp(-x))


def _rmsnorm(x, g):
    return x * lax.rsqrt(jnp.mean(x * x, axis=-1, keepdims=True) + EPS) * g


def _bf16_pieces(x):
    hi = x.astype(_BF16)
    r1 = x - hi.astype(_F32)
    mid = r1.astype(_BF16)
    lo = (r1 - mid.astype(_F32)).astype(_BF16)
    return hi, mid, lo


def _tril_matmul_cumsum(x, tril_bf16):
    hi, mid, lo = _bf16_pieces(x)
    return _dot(tril_bf16, hi) + _dot(tril_bf16, mid) + _dot(tril_bf16, lo)


def _triu_matmul_cumsum(x, triu_bf16):
    hi, mid, lo = _bf16_pieces(x)
    return _dot(hi, triu_bf16) + _dot(mid, triu_bf16) + _dot(lo, triu_bf16)


def _log_sigmoid(x):
    return jnp.minimum(x, 0.0) - jnp.log(1.0 + jnp.exp(-jnp.abs(x)))


def _sublane_cumsum(x, row):
    for sh in (1, 2, 4):
        x = x + jnp.where(row >= sh, pltpu.roll(x, sh, x.ndim - 2), 0.0)
    return x


def _row_at(x, row, t):
    return jnp.sum(jnp.where(row == t, x, 0.0), axis=-2, keepdims=True)


def _gate_vectors_small(igc, lfc, tril, eye):
    lfr = jnp.sum(jnp.where(eye, lfc, 0.0), axis=-2, keepdims=True)
    Fc = jnp.sum(jnp.where(tril, lfr, 0.0), axis=-1, keepdims=True)
    uc = igc - Fc
    ur = jnp.sum(jnp.where(eye, uc, 0.0), axis=-2, keepdims=True)
    return Fc, uc, ur


def _mlstm_head(q, k, v, Fc, uc, ur, C0, n0, m0, tril, row, t_last, fill):
    a = Fc + m0
    D = jnp.where(tril, Fc + ur, -jnp.inf)
    mt = jnp.maximum(a, jnp.max(D, axis=-1, keepdims=True))
    w_inter = jnp.exp(a - mt)
    W = jnp.exp(D - mt)
    fill()
    qb = q.astype(_BF16)
    kb = k.astype(_BF16)
    vb = v.astype(_BF16)
    s = _dot_nt(qb, kb) * W
    fill()
    num = w_inter * _dot(qb, C0.astype(_BF16)) + _dot(s.astype(_BF16), vb)
    den = w_inter * jnp.sum(q * n0, axis=-1, keepdims=True) + jnp.sum(s, axis=-1, keepdims=True)
    h = num / jnp.maximum(jnp.abs(den), jnp.exp(-mt))
    fill()
    F_last = _row_at(Fc, row, t_last)
    m_last = _row_at(mt, row, t_last)
    decay = _row_at(w_inter, row, t_last)
    valid = row <= t_last
    wl = jnp.where(valid, jnp.exp(jnp.where(valid, F_last + uc - m_last, 0.0)), 0.0)
    kw = wl * k
    C_new = decay * C0 + _dot_tn(kw.astype(_BF16), vb)
    n_new = decay * n0 + jnp.sum(kw, axis=-2, keepdims=True)
    return h, C_new, n_new, m_last


def _hgrn_core(q, kk, v, f, b, S0, eye_dh, xor_rc, row, t_last, b_ref, qt_ref, kt_ref, nd_ref, fill):
    n_batch, tt, _ = q.shape
    vb = v.astype(_BF16)
    o = _dot((q * jnp.exp(b)).astype(_BF16), S0.astype(_BF16))
    fill()

    if nd_ref is None:
        fz = jnp.where((row & (SUBLANES - 1)) == 0, 0.0, f)
        o = o + jnp.sum(q * kk, axis=-1, keepdims=True) * v
        E = fz
        for d in range(1, SUBLANES):
            if d > 1:
                E = E * pltpu.roll(fz, d - 1, 1)
            A = jnp.sum(q * pltpu.roll(kk, d, 1) * E, axis=-1, keepdims=True)
            o = o + A * pltpu.roll(v, d, 1)
    else:
        for idx, val in enumerate((q, kk, v, f)):
            nd_ref[idx] = val
        group = SUBLANES * SUBLANES
        for n in range(n_batch):
            for g0 in range(0, tt, group):
                views = [[nd_ref[idx, n, pl.ds(g0 + pos, SUBLANES, stride=SUBLANES), :]
                          for pos in range(SUBLANES)] for idx in range(4)]
                qv, kv, vv, fv = views
                for tp in range(SUBLANES):
                    acc = jnp.sum(qv[tp] * kv[tp], axis=-1, keepdims=True) * vv[tp]
                    E = None
                    for sp in range(tp - 1, -1, -1):
                        E = fv[sp + 1] if E is None else E * fv[sp + 1]
                        A = jnp.sum(qv[tp] * kv[sp] * E, axis=-1, keepdims=True)
                        acc = acc + A * vv[sp]
                    nd_ref[4, n, pl.ds(g0 + tp, SUBLANES, stride=SUBLANES), :] = acc
            fill()
        o = o + nd_ref[4]

    if tt > SUBLANES:
        b_ref[...] = b
        A_far = None
        h = SUBLANES
        while h < tt:
            for j in range(tt // h):
                rows = slice(j * h, (j + 1) * h)
                if j % 2 == 1:
                    b_mid = b_ref[:, pl.ds(j * h - 1, 1), :]
                    qt_ref[:, rows, :] = q[:, rows, :] * jnp.exp(b[:, rows, :] - b_mid)
                    kt_ref[:, rows, :] = jnp.zeros((n_batch, h, DH), _F32)
                else:
                    b_mid = b_ref[:, pl.ds((j + 1) * h - 1, 1), :]
                    kt_ref[:, rows, :] = kk[:, rows, :] * jnp.exp(b_mid - b[:, rows, :])
                    qt_ref[:, rows, :] = jnp.zeros((n_batch, h, DH), _F32)
            A_h = _dot_nt(qt_ref[...].astype(_BF16), kt_ref[...].astype(_BF16))
            if 2 * h < tt:
                A_h = jnp.where(xor_rc < 2 * h, A_h, 0.0)
            A_far = A_h if A_far is None else A_far + A_h
            fill()
            h *= 2
        o = o + _dot(A_far.astype(_BF16), vb)

    bL = _row_at(b, row, t_last)
    valid = row <= t_last
    ks = jnp.where(valid, kk * jnp.exp(jnp.where(valid, bL - b, 0.0)), 0.0)
    bL_col = jnp.sum(jnp.where(eye_dh, bL, 0.0), axis=-1, keepdims=True)
    S_new = jnp.exp(bL_col) * S0 + _dot_tn(ks.astype(_BF16), vb)
    return o, S_new


def _headnorm(h, g):
    return h * lax.rsqrt(jnp.mean(h * h, axis=-1, keepdims=True) + EPS) * g


def _mixers(p_ref, g, convw_ref, convb_ref, mnorm_ref, lbl_ref, hnorm_ref,
            mix_ref, convout_ref, C_in, n_in, m_in, S_in, C_out, n_out, m_out, S_out,
            convbuf, b_ref, qt_ref, kt_ref, nd_ref, tt, t_last, fill=lambda: None):
    batched = len(p_ref.shape) == 3
    n_tiles = p_ref.shape[0] if batched else 1
    lead = (slice(None),) if batched else ()

    def rows(start, size):
        return lead + (pl.ds(start, size), slice(None))

    def cols(start, size):
        return lead + (slice(None), slice(start, start + size))

    def head(hd):
        return lead + (hd,)

    def head_row(hd):
        return lead + (slice(hd, hd + 1), slice(None))

    def cat_heads(get):
        parts = [get(hd) for hd in range(N_HEADS)]
        return jnp.concatenate([x if batched else x[None] for x in parts], axis=0)

    def head_cols(x):
        return cat_heads(lambda hd: x[..., hd * DH:(hd + 1) * DH])

    def head_gain(ref):
        return cat_heads(lambda hd: jnp.broadcast_to(ref[:, hd * DH:(hd + 1) * DH], (n_tiles, 1, DH))
                         if batched else ref[:, hd * DH:(hd + 1) * DH])

    def of_head(x, hd):
        return x[hd * n_tiles:(hd + 1) * n_tiles] if batched else x[hd]

    row = lax.broadcasted_iota(jnp.int32, (tt, 1), 0)
    ri = lax.broadcasted_iota(jnp.int32, (tt, tt), 0)
    ci = lax.broadcasted_iota(jnp.int32, (tt, tt), 1)
    tril = ci <= ri
    eye = ci == ri
    xor_rc = ri ^ ci
    rd = lax.broadcasted_iota(jnp.int32, (DH, DH), 0)
    cd = lax.broadcasted_iota(jnp.int32, (DH, DH), 1)
    eye_dh = rd == cd

    prev = CONV_PAD - (CONV_W - 1)
    last = CONV_PAD + t_last - (CONV_W - 2)
    convbuf[rows(CONV_PAD, tt)] = p_ref[cols(0, 2 * D_A)]
    conv = convb_ref[...]
    for j in range(CONV_W):
        conv = conv + convw_ref[j:j + 1, :] * convbuf[rows(prev + j, tt)]
    new_rows = convbuf[rows(last, CONV_W - 1)]
    convout_ref[...] = new_rows
    convbuf[rows(prev, CONV_W - 1)] = new_rows
    fill()
    qk = conv * _sigmoid(conv)
    fill()
    q = head_cols(qk[..., 0:D_A]) * (DH ** -0.5)
    k = head_cols(qk[..., D_A:2 * D_A])

    base = 4 * D_A
    f_raw = p_ref[cols(base + D_B, D_B)]
    logits = lbl_ref[...]
    ex = jnp.exp(logits - jnp.max(logits, axis=0, keepdims=True))
    lb = ex[0:1, :] / jnp.sum(ex, axis=0, keepdims=True)
    f = lb + (1.0 - lb) * _sigmoid(f_raw)
    lf = jnp.log(f)

    lf_all = _log_sigmoid(g)
    if batched or tt == SUBLANES:
        b = _sublane_cumsum(lf, row)
        Fc, uc, ur = _gate_vectors_small(cat_heads(lambda hd: g[..., hd:hd + 1]),
                                         cat_heads(lambda hd: lf_all[..., N_HEADS + hd:N_HEADS + hd + 1]),
                                         tril, eye)
    else:
        tril_bf16 = jnp.where(tril, 1.0, 0.0).astype(_BF16)
        triu_bf16 = jnp.where(ci >= ri, 1.0, 0.0).astype(_BF16)
        cum = _tril_matmul_cumsum(jnp.concatenate([lf, lf_all], axis=1), tril_bf16)
        b = cum[:, 0:D_B]
        g_t = g.T
        F_rows = _triu_matmul_cumsum(_log_sigmoid(g_t[0:2 * N_HEADS, :]), triu_bf16)
        Fc = cat_heads(lambda hd: cum[:, D_B + N_HEADS + hd:D_B + N_HEADS + hd + 1])
        uc = cat_heads(lambda hd: g[:, hd:hd + 1]) - Fc
        ur = cat_heads(lambda hd: g_t[hd:hd + 1, :] - F_rows[N_HEADS + hd:N_HEADS + hd + 1, :])

    fill()
    v = head_cols(p_ref[cols(2 * D_A, D_A)])
    o_raw = head_cols(p_ref[cols(3 * D_A, D_A)])
    C0 = cat_heads(lambda hd: C_in[head(hd)])
    n0 = cat_heads(lambda hd: n_in[head_row(hd)])
    m0 = cat_heads(m_in)
    h, C_new, n_new, m_new = _mlstm_head(q, k, v, Fc, uc, ur, C0, n0, m0, tril, row, t_last, fill)
    out_a = _sigmoid(o_raw) * _headnorm(h, head_gain(mnorm_ref))
    m_new = jnp.broadcast_to(m_new, m_new.shape[:-1] + (GATE_COLS,))
    for hd in range(N_HEADS):
        C_out[head(hd)] = of_head(C_new, hd)
        n_out[head_row(hd)] = of_head(n_new, hd)
        m_out[head_row(hd)] = of_head(m_new, hd)
        mix_ref[cols(hd * DH, DH)] = of_head(out_a, hd).astype(mix_ref.dtype)

    q_raw = p_ref[cols(base, D_B)]
    g_raw = p_ref[cols(base + 3 * D_B, D_B)]
    kk = (1.0 - lb) * _sigmoid(-f_raw)
    qs = q_raw * _sigmoid(q_raw)
    S0 = cat_heads(lambda hd: S_in[head(hd)])
    o, S_new = _hgrn_core(head_cols(qs), head_cols(kk), head_cols(p_ref[cols(base + 2 * D_B, D_B)]),
                          head_cols(f), head_cols(b), S0, eye_dh, xor_rc, row, t_last,
                          b_ref, qt_ref, kt_ref, nd_ref, fill)
    out_b = _headnorm(o, head_gain(hnorm_ref)) * head_cols(g_raw * _sigmoid(g_raw))
    for hd in range(N_HEADS):
        S_out[head(hd)] = of_head(S_new, hd)
        mix_ref[cols(D_A + hd * DH, DH)] = of_head(out_b, hd).astype(mix_ref.dtype)


def _project(x, nm_ref, wa_ref, wb_ref, bmain_ref, wg_ref, bg_ref):
    h = _rmsnorm(x, nm_ref[...]).astype(_BF16)
    half = P_MAIN // 2
    p = jnp.concatenate([_dot_nt(h, wa_ref[...]) + bmain_ref[:, 0:half],
                         _dot_nt(h, wb_ref[...]) + bmain_ref[:, half:P_MAIN]], axis=1)
    g = _dot_nt(h, wg_ref[...]) + bg_ref[...]
    return p, g


def _layer_kernel(x_ref, xprev_ref, nm_ref, wa_ref, wb_ref, bmain_ref, wg_ref, bg_ref, convw_ref, convb_ref,
                  mnorm_ref, lbl_ref, hnorm_ref, wout_ref, nf_ref, wgate_ref, wup_ref, wdown_ref, nfin_ref,
                  y_ref, convp_ref, C_ref, n_ref, m_ref, S_ref,
                  p_scr, g_scr, mix_scr, convbuf, b_ref, qt_ref, kt_ref, nd_ref, n_tiles, tiles_per_row):
    tt = x_ref.shape[0]
    t = pl.program_id(0)
    ff_edges = (0, D_FF)
    fill_at = (0, 4, 8, 12)

    def ffn_pieces():
        held = {}

        def out_proj():
            x1 = xprev_ref[...] + _dot(mix_scr[...], wout_ref[...])
            held.update(x1=x1, h2=_rmsnorm(x1, nf_ref[...]).astype(_BF16))

        def gate(c):
            cs = pl.ds(ff_edges[c], ff_edges[c + 1] - ff_edges[c])

            def emit():
                held["gate"] = _dot(held["h2"], wgate_ref[:, cs])
            return emit

        def up_down(c):
            cs = pl.ds(ff_edges[c], ff_edges[c + 1] - ff_edges[c])

            def emit():
                gt = held["gate"]
                ff = (gt * _sigmoid(gt) * _dot(held["h2"], wup_ref[:, cs])).astype(_BF16)
                part = _dot(ff, wdown_ref[cs, :])
                held["x2"] = (held["x1"] if c == 0 else held["x2"]) + part
            return emit

        def final():
            y_ref[...] = _rmsnorm(held["x2"], nfin_ref[...])

        pieces = [out_proj]
        for c in range(len(ff_edges) - 1):
            pieces += [gate(c), up_down(c)]
        return pieces + [final]

    @pl.when(t == 0)
    def _():
        mix_scr[...] = jnp.zeros_like(mix_scr)

    @pl.when(jnp.logical_and(t < n_tiles, t % tiles_per_row == 0))
    def _():
        C_ref[...] = jnp.zeros_like(C_ref)
        n_ref[...] = jnp.zeros_like(n_ref)
        m_ref[...] = jnp.zeros_like(m_ref)
        S_ref[...] = jnp.zeros_like(S_ref)
        convbuf[0:CONV_PAD, :] = jnp.zeros((CONV_PAD, 2 * D_A), _F32)

    @pl.when(t < n_tiles)
    def _():
        pieces = ffn_pieces()
        calls = [0]

        def fill():
            for _ in range(fill_at.count(calls[0])):
                pieces.pop(0)()
            calls[0] += 1

        fill()
        p, g = _project(x_ref[...], nm_ref, wa_ref, wb_ref, bmain_ref, wg_ref, bg_ref)
        p_scr[...] = p
        g_scr[...] = g
        ck = CHUNK_PROMPT
        for r0 in range(0, tt, ck):
            _mixers(p_scr.at[pl.ds(r0, ck)], g_scr[pl.ds(r0, ck), :], convw_ref, convb_ref, mnorm_ref, lbl_ref,
                    hnorm_ref, mix_scr.at[pl.ds(r0, ck)], convp_ref, C_ref, n_ref,
                    lambda hd: m_ref[hd:hd + 1, 0:1], S_ref, C_ref, n_ref, m_ref, S_ref,
                    convbuf, b_ref, qt_ref, kt_ref, nd_ref, ck, ck - 1, fill)
        while pieces:
            pieces.pop(0)()

    @pl.when(t == n_tiles)
    def _():
        for piece in ffn_pieces():
            piece()


def _proj_kernel(x_ref, nm_ref, wa_ref, wb_ref, bmain_ref, wg_ref, bg_ref, p_ref, g_ref):
    p, g = _project(x_ref[...], nm_ref, wa_ref, wb_ref, bmain_ref, wg_ref, bg_ref)
    p_ref[...] = p
    g_ref[...] = g


def _sample_kernel(p_ref, g_ref, conv0_ref, C0_ref, n0_ref, m0_ref, S0_ref,
                   convw_ref, convb_ref, mnorm_ref, lbl_ref, hnorm_ref,
                   mix_ref, convs_ref, C_ref, n_ref, m_ref, S_ref,
                   convbuf, t_last):
    n_seq, tt, _ = p_ref.shape
    lead_rows = CONV_PAD - (CONV_W - 1)
    convbuf[:, 0:lead_rows, :] = jnp.zeros((n_seq, lead_rows, 2 * D_A), _F32)
    convbuf[:, lead_rows:CONV_PAD, :] = conv0_ref[...]
    m_ref[...] = jnp.zeros_like(m_ref)
    _mixers(p_ref, g_ref[...], convw_ref, convb_ref, mnorm_ref, lbl_ref, hnorm_ref,
            mix_ref, convs_ref, C0_ref, n0_ref, lambda hd: m0_ref[:, hd:hd + 1, :], S0_ref,
            C_ref, n_ref, m_ref, S_ref, convbuf, None, None, None, None, tt, t_last)


def _ffn_kernel(x_ref, mix_ref, wout_ref, nf_ref, wgate_ref, wup_ref, wdown_ref, nfin_ref, y_ref):
    x1 = x_ref[...] + _dot(mix_ref[...], wout_ref[...])
    h2 = _rmsnorm(x1, nf_ref[...]).astype(_BF16)
    gate = _dot(h2, wgate_ref[...])
    up = _dot(h2, wup_ref[...])
    ff = (gate * _sigmoid(gate) * up).astype(_BF16)
    x2 = x1 + _dot(ff, wdown_ref[...])
    y_ref[...] = _rmsnorm(x2, nfin_ref[...])


def _resident(shape):
    nd = len(shape)
    return pl.BlockSpec(shape, lambda *_: (0,) * nd, pipeline_mode=pl.Buffered(1))


def _prompt_layer(x, mixer_w, ffn_w):
    B, T, D = x.shape
    tt = TT_PROMPT
    ck = CHUNK_PROMPT
    n_tiles = B * T // tt
    tiles_per_row = T // tt
    x2 = x.reshape(B * T, D)
    weights = tuple(mixer_w) + tuple(ffn_w)

    def cur(t):
        return jnp.minimum(t, n_tiles - 1)

    def prev(t):
        return jnp.maximum(t - 1, 0)

    def row(t):
        return cur(t) // tiles_per_row

    out_shape = (
        jax.ShapeDtypeStruct((B * T, D), _F32),
        jax.ShapeDtypeStruct((B, CONV_W - 1, 2 * D_A), _F32),
        jax.ShapeDtypeStruct((B, N_HEADS, DH, DH), _F32),
        jax.ShapeDtypeStruct((B, N_HEADS, DH), _F32),
        jax.ShapeDtypeStruct((B, SUBLANES, GATE_COLS), _F32),
        jax.ShapeDtypeStruct((B, N_HEADS, DH, DH), _F32),
    )
    in_specs = [
        pl.BlockSpec((tt, D), lambda t: (cur(t), 0)),
        pl.BlockSpec((tt, D), lambda t: (prev(t), 0)),
    ] + [_resident(w.shape) for w in weights]
    out_specs = (
        pl.BlockSpec((tt, D), lambda t: (prev(t), 0)),
        pl.BlockSpec((None, CONV_W - 1, 2 * D_A), lambda t: (row(t), 0, 0)),
        pl.BlockSpec((None, N_HEADS, DH, DH), lambda t: (row(t), 0, 0, 0)),
        pl.BlockSpec((None, N_HEADS, DH), lambda t: (row(t), 0, 0)),
        pl.BlockSpec((None, SUBLANES, GATE_COLS), lambda t: (row(t), 0, 0)),
        pl.BlockSpec((None, N_HEADS, DH, DH), lambda t: (row(t), 0, 0, 0)),
    )
    scratch = [
        pltpu.VMEM((tt, P_MAIN), _F32), pltpu.VMEM((tt, GATE_COLS), _F32),
        pltpu.VMEM((tt, D_MODEL), _BF16),
        pltpu.VMEM((CONV_PAD + ck, 2 * D_A), _F32),
        pltpu.VMEM((N_HEADS, ck, DH), _F32),
        pltpu.VMEM((N_HEADS, ck, DH), _F32),
        pltpu.VMEM((N_HEADS, ck, DH), _F32),
        pltpu.VMEM((5, N_HEADS, ck, DH), _F32),
    ]
    return pl.pallas_call(
        functools.partial(_layer_kernel, n_tiles=n_tiles, tiles_per_row=tiles_per_row),
        out_shape=out_shape, grid=(n_tiles + 1,), in_specs=in_specs, out_specs=out_specs,
        scratch_shapes=scratch, name="prompt_layer",
        compiler_params=pltpu.CompilerParams(dimension_semantics=("arbitrary",), vmem_limit_bytes=VMEM_LIMIT),
    )(x2, x2, *weights)


def _sample_proj(x, nm, wa, wb, bmain, wg, bg):
    N, D = x.shape
    tm = min(N, 512)
    return pl.pallas_call(
        _proj_kernel,
        out_shape=(jax.ShapeDtypeStruct((N, P_MAIN), _F32), jax.ShapeDtypeStruct((N, GATE_COLS), _F32)),
        grid=(N // tm,),
        in_specs=[pl.BlockSpec((tm, D), lambda i: (i, 0))] + [_resident(a.shape) for a in (nm, wa, wb, bmain, wg, bg)],
        out_specs=(pl.BlockSpec((tm, P_MAIN), lambda i: (i, 0)), pl.BlockSpec((tm, GATE_COLS), lambda i: (i, 0))),
        name="sample_proj",
        compiler_params=pltpu.CompilerParams(dimension_semantics=("arbitrary",), vmem_limit_bytes=VMEM_LIMIT),
    )(x, nm, wa, wb, bmain, wg, bg)


def _sample_mixer(p, g, conv0, C0, n0, m0, S0, convw, convb, mnorm, lbl, hnorm, t_last):
    B, tt, _ = p.shape
    out_shape = (
        jax.ShapeDtypeStruct((B, tt, D_MODEL), _BF16),
        jax.ShapeDtypeStruct((B, CONV_W - 1, 2 * D_A), _F32),
        jax.ShapeDtypeStruct((B, N_HEADS, DH, DH), _F32),
        jax.ShapeDtypeStruct((B, N_HEADS, DH), _F32),
        jax.ShapeDtypeStruct((B, SUBLANES, GATE_COLS), _F32),
        jax.ShapeDtypeStruct((B, N_HEADS, DH, DH), _F32),
    )
    ns = SEQS_PER_STEP
    in_specs = [
        pl.BlockSpec((ns, tt, P_MAIN), lambda b: (b, 0, 0)),
        pl.BlockSpec((ns, tt, GATE_COLS), lambda b: (b, 0, 0)),
        pl.BlockSpec((ns, CONV_W - 1, 2 * D_A), lambda b: (b, 0, 0)),
        pl.BlockSpec((ns, N_HEADS, DH, DH), lambda b: (b, 0, 0, 0)),
        pl.BlockSpec((ns, N_HEADS, DH), lambda b: (b, 0, 0)),
        pl.BlockSpec((ns, N_HEADS, 1), lambda b: (b, 0, 0)),
        pl.BlockSpec((ns, N_HEADS, DH, DH), lambda b: (b, 0, 0, 0)),
        _resident(convw.shape), _resident(convb.shape), _resident(mnorm.shape), _resident(lbl.shape),
        _resident(hnorm.shape),
    ]
    out_specs = (
        pl.BlockSpec((ns, tt, D_MODEL), lambda b: (b, 0, 0)),
        pl.BlockSpec((ns, CONV_W - 1, 2 * D_A), lambda b: (b, 0, 0)),
        pl.BlockSpec((ns, N_HEADS, DH, DH), lambda b: (b, 0, 0, 0)),
        pl.BlockSpec((ns, N_HEADS, DH), lambda b: (b, 0, 0)),
        pl.BlockSpec((ns, SUBLANES, GATE_COLS), lambda b: (b, 0, 0)),
        pl.BlockSpec((ns, N_HEADS, DH, DH), lambda b: (b, 0, 0, 0)),
    )
    return pl.pallas_call(
        functools.partial(_sample_kernel, t_last=t_last),
        out_shape=out_shape, grid=(B // ns,), in_specs=in_specs, out_specs=out_specs,
        scratch_shapes=[pltpu.VMEM((ns, CONV_PAD + tt, 2 * D_A), _F32)], name="sample_mixer",
        compiler_params=pltpu.CompilerParams(dimension_semantics=("arbitrary",), vmem_limit_bytes=VMEM_LIMIT),
    )(p, g, conv0, C0, n0, m0, S0, convw, convb, mnorm, lbl, hnorm)


def _ffn(x, mix, wout, nf, wgate, wup, wdown, nfin):
    N, D = x.shape
    tm = min(N, TM_FFN)
    return pl.pallas_call(
        _ffn_kernel, out_shape=jax.ShapeDtypeStruct((N, D), _F32), grid=(N // tm,),
        in_specs=[pl.BlockSpec((tm, D), lambda i: (i, 0)), pl.BlockSpec((tm, D_MODEL), lambda i: (i, 0)),
                  _resident(wout.shape), _resident(nf.shape), _resident(wgate.shape), _resident(wup.shape),
                  _resident(wdown.shape), _resident(nfin.shape)],
        out_specs=pl.BlockSpec((tm, D), lambda i: (i, 0)), name="ffn",
        compiler_params=pltpu.CompilerParams(dimension_semantics=("arbitrary",), vmem_limit_bytes=VMEM_LIMIT),
    )(x, mix, wout, nf, wgate, wup, wdown, nfin)


def kernel(x_prompt, x_sample, state_conv, state_mlstm_C, state_mlstm_n, state_mlstm_m, state_hgrn_S,
           norm_mix, w_in, b_in, conv_w, conv_b, mlstm_norm, hgrn_lb_logits, hgrn_norm, w_out,
           norm_ffn, w_gate, w_up, w_down, norm_final):
    depth = w_in.shape[0]
    assert depth == 1, "single-layer stack"
    B, T, D = x_prompt.shape
    BS, TS, _ = x_sample.shape
    l = 0
    g0 = 4 * D_A
    n_gates = 2 * N_HEADS

    wt = jnp.swapaxes(w_in[l], 0, 1)
    wa = wt[:g0].astype(_BF16)
    wb = wt[g0 + n_gates:].astype(_BF16)
    wg = jnp.pad(wt[g0:g0 + n_gates], ((0, GATE_COLS - n_gates), (0, 0))).astype(_BF16)
    bmain = jnp.concatenate([b_in[l, :g0], b_in[l, g0 + n_gates:]])[None, :]
    bg = jnp.pad(b_in[l, g0:g0 + n_gates], (0, GATE_COLS - n_gates))[None, :]
    nm = norm_mix[l][None, :]
    convw = conv_w[l]
    convb = conv_b[l][None, :]
    mnorm = mlstm_norm[l][None, :]
    hnorm = hgrn_norm[l][None, :]
    lbl = hgrn_lb_logits.astype(_F32)
    wout = w_out[l].astype(_BF16)
    wgate = w_gate[l].astype(_BF16)
    wup = w_up[l].astype(_BF16)
    wdown = w_down[l].astype(_BF16)
    nf = norm_ffn[l][None, :]
    nfin = norm_final[None, :]

    mixer_w = (nm, wa, wb, bmain, wg, bg, convw, convb, mnorm, lbl, hnorm)
    ffn_w = (wout, nf, wgate, wup, wdown, nfin)
    y_p, conv_p, C_p, n_p, m_p, S_p = _prompt_layer(x_prompt, mixer_w, ffn_w)
    y_prompt = y_p.reshape(B, T, D)

    xs_pad = jnp.pad(x_sample, ((0, 0), (0, TT_SAMPLE - TS), (0, 0)))
    p_s, g_s = _sample_proj(xs_pad.reshape(BS * TT_SAMPLE, D), nm, wa, wb, bmain, wg, bg)
    mix_s, conv_s, C_s, n_s, m_s, S_s = _sample_mixer(
        p_s.reshape(BS, TT_SAMPLE, P_MAIN), g_s.reshape(BS, TT_SAMPLE, GATE_COLS), state_conv[l],
        state_mlstm_C[l], state_mlstm_n[l], state_mlstm_m[l][:, :, None], state_hgrn_S[l],
        convw, convb, mnorm, lbl, hnorm, TS - 1)
    y_sample = _ffn(x_sample.reshape(BS * TS, D), mix_s[:, :TS, :].reshape(BS * TS, D_MODEL), wout, nf, wgate,
                    wup, wdown, nfin).reshape(BS, TS, D)

    return (y_prompt, y_sample,
            conv_p[None], C_p[None], n_p[None], m_p[None, :, :N_HEADS, 0], S_p[None],
            conv_s[None], C_s[None], n_s[None], m_s[None, :, :N_HEADS, 0], S_s[None])
```

```python
import functools

import jax
import jax.numpy as jnp
from jax import lax
from jax.experimental import pallas as pl
from jax.experimental.pallas import tpu as pltpu

D_MODEL = 1024
D_A = 512
D_B = 512
N_HEADS = 4
DH = 128
CONV_W = 4
D_FF = 2816
EPS = 1e-6
P_MAIN = 2 * D_A + D_A + D_A + 4 * D_B
GATE_COLS = 128
SUBLANES = 8
CONV_PAD = 8
VMEM_LIMIT = 56 * 1024 * 1024

TT_PROMPT = 256
CHUNK_PROMPT = 256
TT_SAMPLE = 8
TM_FFN = 512
SEQS_PER_STEP = 16
FFN_CHUNK_EDGES = (0, D_FF)
FFN_PIECE_POINTS = (0, 4, 8, 12)

_F32 = jnp.float32
_BF16 = jnp.bfloat16


def _dot_dims(a, b, ca, cb):
    batch = ((0,), (0,)) if a.ndim == 3 and b.ndim == 3 else ((), ())
    return lax.dot_general(a, b, (((a.ndim + ca,), (b.ndim + cb,)), batch), preferred_element_type=_F32)


def _dot(a, b):
    return _dot_dims(a, b, -1, -2)


def _dot_nt(a, b):
    return _dot_dims(a, b, -1, -1)


def _dot_tn(a, b):
    return _dot_dims(a, b, -2, -2)


def _sigmoid(x):
    return 1.0 / (1.0 + jnp.exp(-x))


def _rmsnorm(x, g):
    return x * lax.rsqrt(jnp.mean(x * x, axis=-1, keepdims=True) + EPS) * g


def _bf16_pieces(x):
    hi = x.astype(_BF16)
    r1 = x - hi.astype(_F32)
    mid = r1.astype(_BF16)
    lo = (r1 - mid.astype(_F32)).astype(_BF16)
    return hi, mid, lo


def _tril_matmul_cumsum(x, tril_bf16):
    hi, mid, lo = _bf16_pieces(x)
    return _dot(tril_bf16, hi) + _dot(tril_bf16, mid) + _dot(tril_bf16, lo)


def _triu_matmul_cumsum(x, triu_bf16):
    hi, mid, lo = _bf16_pieces(x)
    return _dot(hi, triu_bf16) + _dot(mid, triu_bf16) + _dot(lo, triu_bf16)


def _log_sigmoid(x):
    return jnp.minimum(x, 0.0) - jnp.log(1.0 + jnp.exp(-jnp.abs(x)))


def _sublane_cumsum(x, row):
    for sh in (1, 2, 4):
        x = x + jnp.where(row >= sh, pltpu.roll(x, sh, x.ndim - 2), 0.0)
    return x


def _row_at(x, row, t):
    return jnp.sum(jnp.where(row == t, x, 0.0), axis=-2, keepdims=True)


def _gate_vectors_small(igc, lfc, tril, eye):
    lfr = jnp.sum(jnp.where(eye, lfc, 0.0), axis=-2, keepdims=True)
    Fc = jnp.sum(jnp.where(tril, lfr, 0.0), axis=-1, keepdims=True)
    uc = igc - Fc
    ur = jnp.sum(jnp.where(eye, uc, 0.0), axis=-2, keepdims=True)
    return Fc, uc, ur


def _mlstm_head(q, k, v, Fc, uc, ur, C0, n0, m0, tril, row, t_last, fill):
    a = Fc + m0
    D = jnp.where(tril, Fc + ur, -jnp.inf)
    mt = jnp.maximum(a, jnp.max(D, axis=-1, keepdims=True))
    w_inter = jnp.exp(a - mt)
    W = jnp.exp(D - mt)
    fill()
    qb = q.astype(_BF16)
    kb = k.astype(_BF16)
    vb = v.astype(_BF16)
    s = _dot_nt(qb, kb) * W
    fill()
    num = w_inter * _dot(qb, C0.astype(_BF16)) + _dot(s.astype(_BF16), vb)
    den = w_inter * jnp.sum(q * n0, axis=-1, keepdims=True) + jnp.sum(s, axis=-1, keepdims=True)
    h = num / jnp.maximum(jnp.abs(den), jnp.exp(-mt))
    fill()
    F_last = _row_at(Fc, row, t_last)
    m_last = _row_at(mt, row, t_last)
    decay = _row_at(w_inter, row, t_last)
    valid = row <= t_last
    wl = jnp.where(valid, jnp.exp(jnp.where(valid, F_last + uc - m_last, 0.0)), 0.0)
    kw = wl * k
    C_new = decay * C0 + _dot_tn(kw.astype(_BF16), vb)
    n_new = decay * n0 + jnp.sum(kw, axis=-2, keepdims=True)
    return h, C_new, n_new, m_last


def _hgrn_core(q, kk, v, f, b, S0, eye_dh, xor_rc, row, t_last, b_ref, qt_ref, kt_ref, nd_ref, fill):
    n_batch, tt, _ = q.shape
    vb = v.astype(_BF16)
    o = _dot((q * jnp.exp(b)).astype(_BF16), S0.astype(_BF16))
    fill()

    if nd_ref is None:
        fz = jnp.where((row & (SUBLANES - 1)) == 0, 0.0, f)
        o = o + jnp.sum(q * kk, axis=-1, keepdims=True) * v
        E = fz
        for d in range(1, SUBLANES):
            if d > 1:
                E = E * pltpu.roll(fz, d - 1, 1)
            A = jnp.sum(q * pltpu.roll(kk, d, 1) * E, axis=-1, keepdims=True)
            o = o + A * pltpu.roll(v, d, 1)
    else:
        for idx, val in enumerate((q, kk, v, f)):
            nd_ref[idx] = val
        group = SUBLANES * SUBLANES
        for n in range(n_batch):
            for g0 in range(0, tt, group):
                views = [[nd_ref[idx, n, pl.ds(g0 + pos, SUBLANES, stride=SUBLANES), :]
                          for pos in range(SUBLANES)] for idx in range(4)]
                qv, kv, vv, fv = views
                for tp in range(SUBLANES):
                    acc = jnp.sum(qv[tp] * kv[tp], axis=-1, keepdims=True) * vv[tp]
                    E = None
                    for sp in range(tp - 1, -1, -1):
                        E = fv[sp + 1] if E is None else E * fv[sp + 1]
                        A = jnp.sum(qv[tp] * kv[sp] * E, axis=-1, keepdims=True)
                        acc = acc + A * vv[sp]
                    nd_ref[4, n, pl.ds(g0 + tp, SUBLANES, stride=SUBLANES), :] = acc
            fill()
        o = o + nd_ref[4]

    if tt > SUBLANES:
        b_ref[...] = b
        A_far = None
        h = SUBLANES
        while h < tt:
            for j in range(tt // h):
                rows = slice(j * h, (j + 1) * h)
                if j % 2 == 1:
                    b_mid = b_ref[:, pl.ds(j * h - 1, 1), :]
                    qt_ref[:, rows, :] = q[:, rows, :] * jnp.exp(b[:, rows, :] - b_mid)
                    kt_ref[:, rows, :] = jnp.zeros((n_batch, h, DH), _F32)
                else:
                    b_mid = b_ref[:, pl.ds((j + 1) * h - 1, 1), :]
                    kt_ref[:, rows, :] = kk[:, rows, :] * jnp.exp(b_mid - b[:, rows, :])
                    qt_ref[:, rows, :] = jnp.zeros((n_batch, h, DH), _F32)
            A_h = _dot_nt(qt_ref[...].astype(_BF16), kt_ref[...].astype(_BF16))
            if 2 * h < tt:
                A_h = jnp.where(xor_rc < 2 * h, A_h, 0.0)
            A_far = A_h if A_far is None else A_far + A_h
            fill()
            h *= 2
        o = o + _dot(A_far.astype(_BF16), vb)

    bL = _row_at(b, row, t_last)
    valid = row <= t_last
    ks = jnp.where(valid, kk * jnp.exp(jnp.where(valid, bL - b, 0.0)), 0.0)
    bL_col = jnp.sum(jnp.where(eye_dh, bL, 0.0), axis=-1, keepdims=True)
    S_new = jnp.exp(bL_col) * S0 + _dot_tn(ks.astype(_BF16), vb)
    return o, S_new


def _headnorm(h, g):
    return h * lax.rsqrt(jnp.mean(h * h, axis=-1, keepdims=True) + EPS) * g


def _mixers(p_ref, g, convw_ref, convb_ref, mnorm_ref, lbl_ref, hnorm_ref,
            mix_ref, convout_ref, C_in, n_in, m_in, S_in, C_out, n_out, m_out, S_out,
            convbuf, b_ref, qt_ref, kt_ref, nd_ref, tt, t_last, fill=lambda: None):
    batched = len(p_ref.shape) == 3
    n_tiles = p_ref.shape[0] if batched else 1
    lead = (slice(None),) if batched else ()

    def rows(start, size):
        return lead + (pl.ds(start, size), slice(None))

    def cols(start, size):
        return lead + (slice(None), slice(start, start + size))

    def head(hd):
        return lead + (hd,)

    def head_row(hd):
        return lead + (slice(hd, hd + 1), slice(None))

    def cat_heads(get):
        parts = [get(hd) for hd in range(N_HEADS)]
        return jnp.concatenate([x if batched else x[None] for x in parts], axis=0)

    def head_cols(x):
        return cat_heads(lambda hd: x[..., hd * DH:(hd + 1) * DH])

    def head_gain(ref):
        return cat_heads(lambda hd: jnp.broadcast_to(ref[:, hd * DH:(hd + 1) * DH], (n_tiles, 1, DH))
                         if batched else ref[:, hd * DH:(hd + 1) * DH])

    def of_head(x, hd):
        return x[hd * n_tiles:(hd + 1) * n_tiles] if batched else x[hd]

    row = lax.broadcasted_iota(jnp.int32, (tt, 1), 0)
    ri = lax.broadcasted_iota(jnp.int32, (tt, tt), 0)
    ci = lax.broadcasted_iota(jnp.int32, (tt, tt), 1)
    tril = ci <= ri
    eye = ci == ri
    xor_rc = ri ^ ci
    rd = lax.broadcasted_iota(jnp.int32, (DH, DH), 0)
    cd = lax.broadcasted_iota(jnp.int32, (DH, DH), 1)
    eye_dh = rd == cd

    prev = CONV_PAD - (CONV_W - 1)
    last = CONV_PAD + t_last - (CONV_W - 2)
    convbuf[rows(CONV_PAD, tt)] = p_ref[cols(0, 2 * D_A)]
    conv = convb_ref[...]
    for j in range(CONV_W):
        conv = conv + convw_ref[j:j + 1, :] * convbuf[rows(prev + j, tt)]
    new_rows = convbuf[rows(last, CONV_W - 1)]
    convout_ref[...] = new_rows
    convbuf[rows(prev, CONV_W - 1)] = new_rows
    fill()
    qk = conv * _sigmoid(conv)
    fill()
    q = head_cols(qk[..., 0:D_A]) * (DH ** -0.5)
    k = head_cols(qk[..., D_A:2 * D_A])

    base = 4 * D_A
    f_raw = p_ref[cols(base + D_B, D_B)]
    logits = lbl_ref[...]
    ex = jnp.exp(logits - jnp.max(logits, axis=0, keepdims=True))
    lb = ex[0:1, :] / jnp.sum(ex, axis=0, keepdims=True)
    f = lb + (1.0 - lb) * _sigmoid(f_raw)
    lf = jnp.log(f)

    lf_all = _log_sigmoid(g)
    if batched or tt == SUBLANES:
        b = _sublane_cumsum(lf, row)
        Fc, uc, ur = _gate_vectors_small(cat_heads(lambda hd: g[..., hd:hd + 1]),
                                         cat_heads(lambda hd: lf_all[..., N_HEADS + hd:N_HEADS + hd + 1]),
                                         tril, eye)
    else:
        tril_bf16 = jnp.where(tril, 1.0, 0.0).astype(_BF16)
        triu_bf16 = jnp.where(ci >= ri, 1.0, 0.0).astype(_BF16)
        cum = _tril_matmul_cumsum(jnp.concatenate([lf, lf_all], axis=1), tril_bf16)
        b = cum[:, 0:D_B]
        g_t = g.T
        F_rows = _triu_matmul_cumsum(_log_sigmoid(g_t[0:2 * N_HEADS, :]), triu_bf16)
        Fc = cat_heads(lambda hd: cum[:, D_B + N_HEADS + hd:D_B + N_HEADS + hd + 1])
        uc = cat_heads(lambda hd: g[:, hd:hd + 1]) - Fc
        ur = cat_heads(lambda hd: g_t[hd:hd + 1, :] - F_rows[N_HEADS + hd:N_HEADS + hd + 1, :])

    fill()
    v = head_cols(p_ref[cols(2 * D_A, D_A)])
    o_raw = head_cols(p_ref[cols(3 * D_A, D_A)])
    C0 = cat_heads(lambda hd: C_in[head(hd)])
    n0 = cat_heads(lambda hd: n_in[head_row(hd)])
    m0 = cat_heads(m_in)
    h, C_new, n_new, m_new = _mlstm_head(q, k, v, Fc, uc, ur, C0, n0, m0, tril, row, t_last, fill)
    out_a = _sigmoid(o_raw) * _headnorm(h, head_gain(mnorm_ref))
    m_new = jnp.broadcast_to(m_new, m_new.shape[:-1] + (GATE_COLS,))
    for hd in range(N_HEADS):
        C_out[head(hd)] = of_head(C_new, hd)
        n_out[head_row(hd)] = of_head(n_new, hd)
        m_out[head_row(hd)] = of_head(m_new, hd)
        mix_ref[cols(hd * DH, DH)] = of_head(out_a, hd).astype(mix_ref.dtype)

    q_raw = p_ref[cols(base, D_B)]
    g_raw = p_ref[cols(base + 3 * D_B, D_B)]
    kk = (1.0 - lb) * _sigmoid(-f_raw)
    qs = q_raw * _sigmoid(q_raw)
    S0 = cat_heads(lambda hd: S_in[head(hd)])
    o, S_new = _hgrn_core(head_cols(qs), head_cols(kk), head_cols(p_ref[cols(base + 2 * D_B, D_B)]),
                          head_cols(f), head_cols(b), S0, eye_dh, xor_rc, row, t_last,
                          b_ref, qt_ref, kt_ref, nd_ref, fill)
    out_b = _headnorm(o, head_gain(hnorm_ref)) * head_cols(g_raw * _sigmoid(g_raw))
    for hd in range(N_HEADS):
        S_out[head(hd)] = of_head(S_new, hd)
        mix_ref[cols(D_A + hd * DH, DH)] = of_head(out_b, hd).astype(mix_ref.dtype)


def _project(x, nm_ref, wa_ref, wb_ref, bmain_ref, wg_ref, bg_ref):
    h = _rmsnorm(x, nm_ref[...]).astype(_BF16)
    half = P_MAIN // 2
    p = jnp.concatenate([_dot_nt(h, wa_ref[...]) + bmain_ref[:, 0:half],
                         _dot_nt(h, wb_ref[...]) + bmain_ref[:, half:P_MAIN]], axis=1)
    g = _dot_nt(h, wg_ref[...]) + bg_ref[...]
    return p, g


def _layer_kernel(x_ref, xprev_ref, nm_ref, wa_ref, wb_ref, bmain_ref, wg_ref, bg_ref, convw_ref, convb_ref,
                  mnorm_ref, lbl_ref, hnorm_ref, wout_ref, nf_ref, wgate_ref, wup_ref, wdown_ref, nfin_ref,
                  y_ref, convp_ref, C_ref, n_ref, m_ref, S_ref,
                  p_scr, g_scr, mix_scr, convbuf, b_ref, qt_ref, kt_ref, nd_ref, n_tiles, tiles_per_row):
    tt = x_ref.shape[0]
    t = pl.program_id(0)
    ff_edges = FFN_CHUNK_EDGES
    fill_at = FFN_PIECE_POINTS

    def ffn_pieces():
        held = {}

        def out_proj():
            x1 = xprev_ref[...] + _dot(mix_scr[...], wout_ref[...])
            held.update(x1=x1, h2=_rmsnorm(x1, nf_ref[...]).astype(_BF16))

        def gate(c):
            cs = pl.ds(ff_edges[c], ff_edges[c + 1] - ff_edges[c])

            def emit():
                held["gate"] = _dot(held["h2"], wgate_ref[:, cs])
            return emit

        def up_down(c):
            cs = pl.ds(ff_edges[c], ff_edges[c + 1] - ff_edges[c])

            def emit():
                gt = held["gate"]
                ff = (gt * _sigmoid(gt) * _dot(held["h2"], wup_ref[:, cs])).astype(_BF16)
                part = _dot(ff, wdown_ref[cs, :])
                held["x2"] = (held["x1"] if c == 0 else held["x2"]) + part
            return emit

        def final():
            y_ref[...] = _rmsnorm(held["x2"], nfin_ref[...])

        pieces = [out_proj]
        for c in range(len(ff_edges) - 1):
            pieces += [gate(c), up_down(c)]
        return pieces + [final]

    @pl.when(t == 0)
    def _():
        mix_scr[...] = jnp.zeros_like(mix_scr)

    @pl.when(jnp.logical_and(t < n_tiles, t % tiles_per_row == 0))
    def _():
        C_ref[...] = jnp.zeros_like(C_ref)
        n_ref[...] = jnp.zeros_like(n_ref)
        m_ref[...] = jnp.zeros_like(m_ref)
        S_ref[...] = jnp.zeros_like(S_ref)
        convbuf[0:CONV_PAD, :] = jnp.zeros((CONV_PAD, 2 * D_A), _F32)

    @pl.when(t < n_tiles)
    def _():
        pieces = ffn_pieces()
        calls = [0]

        def fill():
            for _ in range(fill_at.count(calls[0])):
                pieces.pop(0)()
            calls[0] += 1

        fill()
        p, g = _project(x_ref[...], nm_ref, wa_ref, wb_ref, bmain_ref, wg_ref, bg_ref)
        p_scr[...] = p
        g_scr[...] = g
        ck = CHUNK_PROMPT
        for r0 in range(0, tt, ck):
            _mixers(p_scr.at[pl.ds(r0, ck)], g_scr[pl.ds(r0, ck), :], convw_ref, convb_ref, mnorm_ref, lbl_ref,
                    hnorm_ref, mix_scr.at[pl.ds(r0, ck)], convp_ref, C_ref, n_ref,
                    lambda hd: m_ref[hd:hd + 1, 0:1], S_ref, C_ref, n_ref, m_ref, S_ref,
                    convbuf, b_ref, qt_ref, kt_ref, nd_ref, ck, ck - 1, fill)
        while pieces:
            pieces.pop(0)()

    @pl.when(t == n_tiles)
    def _():
        for piece in ffn_pieces():
            piece()


def _proj_kernel(x_ref, nm_ref, wa_ref, wb_ref, bmain_ref, wg_ref, bg_ref, p_ref, g_ref):
    p, g = _project(x_ref[...], nm_ref, wa_ref, wb_ref, bmain_ref, wg_ref, bg_ref)
    p_ref[...] = p
    g_ref[...] = g


def _sample_kernel(p_ref, g_ref, conv0_ref, C0_ref, n0_ref, m0_ref, S0_ref,
                   convw_ref, convb_ref, mnorm_ref, lbl_ref, hnorm_ref,
                   mix_ref, convs_ref, C_ref, n_ref, m_ref, S_ref,
                   convbuf, t_last):
    n_seq, tt, _ = p_ref.shape
    lead_rows = CONV_PAD - (CONV_W - 1)
    convbuf[:, 0:lead_rows, :] = jnp.zeros((n_seq, lead_rows, 2 * D_A), _F32)
    convbuf[:, lead_rows:CONV_PAD, :] = conv0_ref[...]
    m_ref[...] = jnp.zeros_like(m_ref)
    _mixers(p_ref, g_ref[...], convw_ref, convb_ref, mnorm_ref, lbl_ref, hnorm_ref,
            mix_ref, convs_ref, C0_ref, n0_ref, lambda hd: m0_ref[:, hd:hd + 1, :], S0_ref,
            C_ref, n_ref, m_ref, S_ref, convbuf, None, None, None, None, tt, t_last)


def _ffn_kernel(x_ref, mix_ref, wout_ref, nf_ref, wgate_ref, wup_ref, wdown_ref, nfin_ref, y_ref):
    x1 = x_ref[...] + _dot(mix_ref[...], wout_ref[...])
    h2 = _rmsnorm(x1, nf_ref[...]).astype(_BF16)
    gate = _dot(h2, wgate_ref[...])
    up = _dot(h2, wup_ref[...])
    ff = (gate * _sigmoid(gate) * up).astype(_BF16)
    x2 = x1 + _dot(ff, wdown_ref[...])
    y_ref[...] = _rmsnorm(x2, nfin_ref[...])


def _resident(shape):
    nd = len(shape)
    return pl.BlockSpec(shape, lambda *_: (0,) * nd, pipeline_mode=pl.Buffered(1))


def _prompt_layer(x, mixer_w, ffn_w):
    B, T, D = x.shape
    tt = TT_PROMPT
    ck = CHUNK_PROMPT
    n_tiles = B * T // tt
    tiles_per_row = T // tt
    x2 = x.reshape(B * T, D)
    weights = tuple(mixer_w) + tuple(ffn_w)

    def cur(t):
        return jnp.minimum(t, n_tiles - 1)

    def prev(t):
        return jnp.maximum(t - 1, 0)

    def row(t):
        return cur(t) // tiles_per_row

    out_shape = (
        jax.ShapeDtypeStruct((B * T, D), _F32),
        jax.ShapeDtypeStruct((B, CONV_W - 1, 2 * D_A), _F32),
        jax.ShapeDtypeStruct((B, N_HEADS, DH, DH), _F32),
        jax.ShapeDtypeStruct((B, N_HEADS, DH), _F32),
        jax.ShapeDtypeStruct((B, SUBLANES, GATE_COLS), _F32),
        jax.ShapeDtypeStruct((B, N_HEADS, DH, DH), _F32),
    )
    in_specs = [
        pl.BlockSpec((tt, D), lambda t: (cur(t), 0)),
        pl.BlockSpec((tt, D), lambda t: (prev(t), 0)),
    ] + [_resident(w.shape) for w in weights]
    out_specs = (
        pl.BlockSpec((tt, D), lambda t: (prev(t), 0)),
        pl.BlockSpec((None, CONV_W - 1, 2 * D_A), lambda t: (row(t), 0, 0)),
        pl.BlockSpec((None, N_HEADS, DH, DH), lambda t: (row(t), 0, 0, 0)),
        pl.BlockSpec((None, N_HEADS, DH), lambda t: (row(t), 0, 0)),
        pl.BlockSpec((None, SUBLANES, GATE_COLS), lambda t: (row(t), 0, 0)),
        pl.BlockSpec((None, N_HEADS, DH, DH), lambda t: (row(t), 0, 0, 0)),
    )
    scratch = [
        pltpu.VMEM((tt, P_MAIN), _F32), pltpu.VMEM((tt, GATE_COLS), _F32),
        pltpu.VMEM((tt, D_MODEL), _BF16),
        pltpu.VMEM((CONV_PAD + ck, 2 * D_A), _F32),
        pltpu.VMEM((N_HEADS, ck, DH), _F32),
        pltpu.VMEM((N_HEADS, ck, DH), _F32),
        pltpu.VMEM((N_HEADS, ck, DH), _F32),
        pltpu.VMEM((5, N_HEADS, ck, DH), _F32),
    ]
    return pl.pallas_call(
        functools.partial(_layer_kernel, n_tiles=n_tiles, tiles_per_row=tiles_per_row),
        out_shape=out_shape, grid=(n_tiles + 1,), in_specs=in_specs, out_specs=out_specs,
        scratch_shapes=scratch, name="prompt_layer",
        compiler_params=pltpu.CompilerParams(dimension_semantics=("arbitrary",), vmem_limit_bytes=VMEM_LIMIT),
    )(x2, x2, *weights)


def _sample_proj(x, nm, wa, wb, bmain, wg, bg):
    N, D = x.shape
    tm = min(N, 512)
    return pl.pallas_call(
        _proj_kernel,
        out_shape=(jax.ShapeDtypeStruct((N, P_MAIN), _F32), jax.ShapeDtypeStruct((N, GATE_COLS), _F32)),
        grid=(N // tm,),
        in_specs=[pl.BlockSpec((tm, D), lambda i: (i, 0))] + [_resident(a.shape) for a in (nm, wa, wb, bmain, wg, bg)],
        out_specs=(pl.BlockSpec((tm, P_MAIN), lambda i: (i, 0)), pl.BlockSpec((tm, GATE_COLS), lambda i: (i, 0))),
        name="sample_proj",
        compiler_params=pltpu.CompilerParams(dimension_semantics=("arbitrary",), vmem_limit_bytes=VMEM_LIMIT),
    )(x, nm, wa, wb, bmain, wg, bg)


def _sample_mixer(p, g, conv0, C0, n0, m0, S0, convw, convb, mnorm, lbl, hnorm, t_last):
    B, tt, _ = p.shape
    out_shape = (
        jax.ShapeDtypeStruct((B, tt, D_MODEL), _BF16),
        jax.ShapeDtypeStruct((B, CONV_W - 1, 2 * D_A), _F32),
        jax.ShapeDtypeStruct((B, N_HEADS, DH, DH), _F32),
        jax.ShapeDtypeStruct((B, N_HEADS, DH), _F32),
        jax.ShapeDtypeStruct((B, SUBLANES, GATE_COLS), _F32),
        jax.ShapeDtypeStruct((B, N_HEADS, DH, DH), _F32),
    )
    ns = SEQS_PER_STEP
    in_specs = [
        pl.BlockSpec((ns, tt, P_MAIN), lambda b: (b, 0, 0)),
        pl.BlockSpec((ns, tt, GATE_COLS), lambda b: (b, 0, 0)),
        pl.BlockSpec((ns, CONV_W - 1, 2 * D_A), lambda b: (b, 0, 0)),
        pl.BlockSpec((ns, N_HEADS, DH, DH), lambda b: (b, 0, 0, 0)),
        pl.BlockSpec((ns, N_HEADS, DH), lambda b: (b, 0, 0)),
        pl.BlockSpec((ns, N_HEADS, 1), lambda b: (b, 0, 0)),
        pl.BlockSpec((ns, N_HEADS, DH, DH), lambda b: (b, 0, 0, 0)),
        _resident(convw.shape), _resident(convb.shape), _resident(mnorm.shape), _resident(lbl.shape),
        _resident(hnorm.shape),
    ]
    out_specs = (
        pl.BlockSpec((ns, tt, D_MODEL), lambda b: (b, 0, 0)),
        pl.BlockSpec((ns, CONV_W - 1, 2 * D_A), lambda b: (b, 0, 0)),
        pl.BlockSpec((ns, N_HEADS, DH, DH), lambda b: (b, 0, 0, 0)),
        pl.BlockSpec((ns, N_HEADS, DH), lambda b: (b, 0, 0)),
        pl.BlockSpec((ns, SUBLANES, GATE_COLS), lambda b: (b, 0, 0)),
        pl.BlockSpec((ns, N_HEADS, DH, DH), lambda b: (b, 0, 0, 0)),
    )
    return pl.pallas_call(
        functools.partial(_sample_kernel, t_last=t_last),
        out_shape=out_shape, grid=(B // ns,), in_specs=in_specs, out_specs=out_specs,
        scratch_shapes=[pltpu.VMEM((ns, CONV_PAD + tt, 2 * D_A), _F32)], name="sample_mixer",
        compiler_params=pltpu.CompilerParams(dimension_semantics=("arbitrary",), vmem_limit_bytes=VMEM_LIMIT),
    )(p, g, conv0, C0, n0, m0, S0, convw, convb, mnorm, lbl, hnorm)


def _ffn(x, mix, wout, nf, wgate, wup, wdown, nfin):
    N, D = x.shape
    tm = min(N, TM_FFN)
    return pl.pallas_call(
        _ffn_kernel, out_shape=jax.ShapeDtypeStruct((N, D), _F32), grid=(N // tm,),
        in_specs=[pl.BlockSpec((tm, D), lambda i: (i, 0)), pl.BlockSpec((tm, D_MODEL), lambda i: (i, 0)),
                  _resident(wout.shape), _resident(nf.shape), _resident(wgate.shape), _resident(wup.shape),
                  _resident(wdown.shape), _resident(nfin.shape)],
        out_specs=pl.BlockSpec((tm, D), lambda i: (i, 0)), name="ffn",
        compiler_params=pltpu.CompilerParams(dimension_semantics=("arbitrary",), vmem_limit_bytes=VMEM_LIMIT),
    )(x, mix, wout, nf, wgate, wup, wdown, nfin)


def kernel(x_prompt, x_sample, state_conv, state_mlstm_C, state_mlstm_n, state_mlstm_m, state_hgrn_S,
           norm_mix, w_in, b_in, conv_w, conv_b, mlstm_norm, hgrn_lb_logits, hgrn_norm, w_out,
           norm_ffn, w_gate, w_up, w_down, norm_final):
    depth = w_in.shape[0]
    assert depth == 1, "single-layer stack"
    B, T, D = x_prompt.shape
    BS, TS, _ = x_sample.shape
    l = 0
    g0 = 4 * D_A
    n_gates = 2 * N_HEADS

    wt = jnp.swapaxes(w_in[l], 0, 1)
    wa = wt[:g0].astype(_BF16)
    wb = wt[g0 + n_gates:].astype(_BF16)
    wg = jnp.pad(wt[g0:g0 + n_gates], ((0, GATE_COLS - n_gates), (0, 0))).astype(_BF16)
    bmain = jnp.concatenate([b_in[l, :g0], b_in[l, g0 + n_gates:]])[None, :]
    bg = jnp.pad(b_in[l, g0:g0 + n_gates], (0, GATE_COLS - n_gates))[None, :]
    nm = norm_mix[l][None, :]
    convw = conv_w[l]
    convb = conv_b[l][None, :]
    mnorm = mlstm_norm[l][None, :]
    hnorm = hgrn_norm[l][None, :]
    lbl = hgrn_lb_logits.astype(_F32)
    wout = w_out[l].astype(_BF16)
    wgate = w_gate[l].astype(_BF16)
    wup = w_up[l].astype(_BF16)
    wdown = w_down[l].astype(_BF16)
    nf = norm_ffn[l][None, :]
    nfin = norm_final[None, :]

    mixer_w = (nm, wa, wb, bmain, wg, bg, convw, convb, mnorm, lbl, hnorm)
    ffn_w = (wout, nf, wgate, wup, wdown, nfin)
    y_p, conv_p, C_p, n_p, m_p, S_p = _prompt_layer(x_prompt, mixer_w, ffn_w)
    y_prompt = y_p.reshape(B, T, D)

    xs_pad = jnp.pad(x_sample, ((0, 0), (0, TT_SAMPLE - TS), (0, 0)))
    p_s, g_s = _sample_proj(xs_pad.reshape(BS * TT_SAMPLE, D), nm, wa, wb, bmain, wg, bg)
    mix_s, conv_s, C_s, n_s, m_s, S_s = _sample_mixer(
        p_s.reshape(BS, TT_SAMPLE, P_MAIN), g_s.reshape(BS, TT_SAMPLE, GATE_COLS), state_conv[l],
        state_mlstm_C[l], state_mlstm_n[l], state_mlstm_m[l][:, :, None], state_hgrn_S[l],
        convw, convb, mnorm, lbl, hnorm, TS - 1)
    y_sample = _ffn(x_sample.reshape(BS * TS, D), mix_s[:, :TS, :].reshape(BS * TS, D_MODEL), wout, nf, wgate,
                    wup, wdown, nfin).reshape(BS, TS, D)

    return (y_prompt, y_sample,
            conv_p[None], C_p[None], n_p[None], m_p[None, :, :N_HEADS, 0], S_p[None],
            conv_s[None], C_s[None], n_s[None], m_s[None, :, :N_HEADS, 0], S_s[None])
```

```python
import functools

import jax
import jax.numpy as jnp
from jax import lax
from jax.experimental import pallas as pl
from jax.experimental.pallas import tpu as pltpu

D_MODEL = 1024
D_A = 512
D_B = 512
N_HEADS = 4
DH = 128
CONV_W = 4
D_FF = 2816
EPS = 1e-6
P_MAIN = 2 * D_A + D_A + D_A + 4 * D_B
GATE_COLS = 128
SUBLANES = 8
CONV_PAD = 8
VMEM_LIMIT = 56 * 1024 * 1024

TT_PROMPT = 256
CHUNK_PROMPT = 256
TT_SAMPLE = 8
TM_FFN = 512
SEQS_PER_STEP = 16
FFN_CHUNK_EDGES = (0, D_FF)
FFN_PIECE_POINTS = (0, 4, 8, 12)

_F32 = jnp.float32
_BF16 = jnp.bfloat16


def _dot_dims(a, b, ca, cb):
    batch = ((0,), (0,)) if a.ndim == 3 and b.ndim == 3 else ((), ())
    return lax.dot_general(a, b, (((a.ndim + ca,), (b.ndim + cb,)), batch), preferred_element_type=_F32)


def _dot(a, b):
    return _dot_dims(a, b, -1, -2)


def _dot_nt(a, b):
    return _dot_dims(a, b, -1, -1)


def _dot_tn(a, b):
    return _dot_dims(a, b, -2, -2)


def _sigmoid(x):
    return 1.0 / (1.0 + jnp.exp(-x))


def _rmsnorm(x, g):
    return x * lax.rsqrt(jnp.mean(x * x, axis=-1, keepdims=True) + EPS) * g


def _bf16_pieces(x):
    hi = x.astype(_BF16)
    r1 = x - hi.astype(_F32)
    mid = r1.astype(_BF16)
    lo = (r1 - mid.astype(_F32)).astype(_BF16)
    return hi, mid, lo


def _tril_matmul_cumsum(x, tril_bf16):
    hi, mid, lo = _bf16_pieces(x)
    return _dot(tril_bf16, hi) + _dot(tril_bf16, mid) + _dot(tril_bf16, lo)


def _triu_matmul_cumsum(x, triu_bf16):
    hi, mid, lo = _bf16_pieces(x)
    return _dot(hi, triu_bf16) + _dot(mid, triu_bf16) + _dot(lo, triu_bf16)


def _log_sigmoid(x):
    return jnp.minimum(x, 0.0) - jnp.log1p(jnp.exp(-jnp.abs(x)))


def _sublane_cumsum(x, row):
    for sh in (1, 2, 4):
        x = x + jnp.where(row >= sh, pltpu.roll(x, sh, x.ndim - 2), 0.0)
    return x


def _row_at(x, row, t):
    return jnp.sum(jnp.where(row == t, x, 0.0), axis=-2, keepdims=True)


def _gate_vectors_small(igc, lfc, tril, eye):
    lfr = jnp.sum(jnp.where(eye, lfc, 0.0), axis=-2, keepdims=True)
    Fc = jnp.sum(jnp.where(tril, lfr, 0.0), axis=-1, keepdims=True)
    uc = igc - Fc
    ur = jnp.sum(jnp.where(eye, uc, 0.0), axis=-2, keepdims=True)
    return Fc, uc, ur


def _mlstm_head(q, k, v, Fc, uc, ur, C0, n0, m0, tril, row, t_last, fill):
    a = Fc + m0
    D = jnp.where(tril, Fc + ur, -jnp.inf)
    mt = jnp.maximum(a, jnp.max(D, axis=-1, keepdims=True))
    w_inter = jnp.exp(a - mt)
    W = jnp.exp(D - mt)
    fill()
    qb = q.astype(_BF16)
    kb = k.astype(_BF16)
    vb = v.astype(_BF16)
    s = _dot_nt(qb, kb) * W
    fill()
    num = w_inter * _dot(qb, C0.astype(_BF16)) + _dot(s.astype(_BF16), vb)
    den = w_inter * jnp.sum(q * n0, axis=-1, keepdims=True) + jnp.sum(s, axis=-1, keepdims=True)
    h = num / jnp.maximum(jnp.abs(den), jnp.exp(-mt))
    fill()
    F_last = _row_at(Fc, row, t_last)
    m_last = _row_at(mt, row, t_last)
    decay = _row_at(w_inter, row, t_last)
    valid = row <= t_last
    wl = jnp.where(valid, jnp.exp(jnp.where(valid, F_last + uc - m_last, 0.0)), 0.0)
    kw = wl * k
    C_new = decay * C0 + _dot_tn(kw.astype(_BF16), vb)
    n_new = decay * n0 + jnp.sum(kw, axis=-2, keepdims=True)
    return h, C_new, n_new, m_last


def _hgrn_core(q, kk, v, f, b, S0, eye_dh, xor_rc, row, t_last, b_ref, qt_ref, kt_ref, nd_ref, fill):
    n_batch, tt, _ = q.shape
    vb = v.astype(_BF16)
    o = _dot((q * jnp.exp(b)).astype(_BF16), S0.astype(_BF16))
    fill()

    if nd_ref is None:
        fz = jnp.where((row & (SUBLANES - 1)) == 0, 0.0, f)
        o = o + jnp.sum(q * kk, axis=-1, keepdims=True) * v
        E = fz
        for d in range(1, SUBLANES):
            if d > 1:
                E = E * pltpu.roll(fz, d - 1, 1)
            A = jnp.sum(q * pltpu.roll(kk, d, 1) * E, axis=-1, keepdims=True)
            o = o + A * pltpu.roll(v, d, 1)
    else:
        for idx, val in enumerate((q, kk, v, f)):
            nd_ref[idx] = val
        group = SUBLANES * SUBLANES
        for n in range(n_batch):
            for g0 in range(0, tt, group):
                views = [[nd_ref[idx, n, pl.ds(g0 + pos, SUBLANES, stride=SUBLANES), :]
                          for pos in range(SUBLANES)] for idx in range(4)]
                qv, kv, vv, fv = views
                for tp in range(SUBLANES):
                    acc = jnp.sum(qv[tp] * kv[tp], axis=-1, keepdims=True) * vv[tp]
                    E = None
                    for sp in range(tp - 1, -1, -1):
                        E = fv[sp + 1] if E is None else E * fv[sp + 1]
                        A = jnp.sum(qv[tp] * kv[sp] * E, axis=-1, keepdims=True)
                        acc = acc + A * vv[sp]
                    nd_ref[4, n, pl.ds(g0 + tp, SUBLANES, stride=SUBLANES), :] = acc
            fill()
        o = o + nd_ref[4]

    if tt > SUBLANES:
        b_ref[...] = b
        A_far = None
        h = SUBLANES
        while h < tt:
            for j in range(tt // h):
                rows = slice(j * h, (j + 1) * h)
                if j % 2 == 1:
                    b_mid = b_ref[:, pl.ds(j * h - 1, 1), :]
                    qt_ref[:, rows, :] = q[:, rows, :] * jnp.exp(b[:, rows, :] - b_mid)
                    kt_ref[:, rows, :] = jnp.zeros((n_batch, h, DH), _F32)
                else:
                    b_mid = b_ref[:, pl.ds((j + 1) * h - 1, 1), :]
                    kt_ref[:, rows, :] = kk[:, rows, :] * jnp.exp(b_mid - b[:, rows, :])
                    qt_ref[:, rows, :] = jnp.zeros((n_batch, h, DH), _F32)
            A_h = _dot_nt(qt_ref[...].astype(_BF16), kt_ref[...].astype(_BF16))
            if 2 * h < tt:
                A_h = jnp.where(xor_rc < 2 * h, A_h, 0.0)
            A_far = A_h if A_far is None else A_far + A_h
            fill()
            h *= 2
        o = o + _dot(A_far.astype(_BF16), vb)

    bL = _row_at(b, row, t_last)
    valid = row <= t_last
    ks = jnp.where(valid, kk * jnp.exp(jnp.where(valid, bL - b, 0.0)), 0.0)
    bL_col = jnp.sum(jnp.where(eye_dh, bL, 0.0), axis=-1, keepdims=True)
    S_new = jnp.exp(bL_col) * S0 + _dot_tn(ks.astype(_BF16), vb)
    return o, S_new


def _headnorm(h, g):
    return h * lax.rsqrt(jnp.mean(h * h, axis=-1, keepdims=True) + EPS) * g


def _mixers(p_ref, g, convw_ref, convb_ref, mnorm_ref, lbl_ref, hnorm_ref,
            mix_ref, convout_ref, C_in, n_in, m_in, S_in, C_out, n_out, m_out, S_out,
            convbuf, b_ref, qt_ref, kt_ref, nd_ref, tt, t_last, fill=lambda: None):
    batched = len(p_ref.shape) == 3
    n_tiles = p_ref.shape[0] if batched else 1
    lead = (slice(None),) if batched else ()

    def rows(start, size):
        return lead + (pl.ds(start, size), slice(None))

    def cols(start, size):
        return lead + (slice(None), slice(start, start + size))

    def head(hd):
        return lead + (hd,)

    def head_row(hd):
        return lead + (slice(hd, hd + 1), slice(None))

    def cat_heads(get):
        parts = [get(hd) for hd in range(N_HEADS)]
        return jnp.concatenate([x if batched else x[None] for x in parts], axis=0)

    def head_cols(x):
        return cat_heads(lambda hd: x[..., hd * DH:(hd + 1) * DH])

    def head_gain(ref):
        return cat_heads(lambda hd: jnp.broadcast_to(ref[:, hd * DH:(hd + 1) * DH], (n_tiles, 1, DH))
                         if batched else ref[:, hd * DH:(hd + 1) * DH])

    def of_head(x, hd):
        return x[hd * n_tiles:(hd + 1) * n_tiles] if batched else x[hd]

    row = lax.broadcasted_iota(jnp.int32, (tt, 1), 0)
    ri = lax.broadcasted_iota(jnp.int32, (tt, tt), 0)
    ci = lax.broadcasted_iota(jnp.int32, (tt, tt), 1)
    tril = ci <= ri
    eye = ci == ri
    xor_rc = ri ^ ci
    rd = lax.broadcasted_iota(jnp.int32, (DH, DH), 0)
    cd = lax.broadcasted_iota(jnp.int32, (DH, DH), 1)
    eye_dh = rd == cd

    prev = CONV_PAD - (CONV_W - 1)
    last = CONV_PAD + t_last - (CONV_W - 2)
    convbuf[rows(CONV_PAD, tt)] = p_ref[cols(0, 2 * D_A)]
    conv = convb_ref[...]
    for j in range(CONV_W):
        conv = conv + convw_ref[j:j + 1, :] * convbuf[rows(prev + j, tt)]
    new_rows = convbuf[rows(last, CONV_W - 1)]
    convout_ref[...] = new_rows
    convbuf[rows(prev, CONV_W - 1)] = new_rows
    fill()
    qk = conv * _sigmoid(conv)
    fill()
    q = head_cols(qk[..., 0:D_A]) * (DH ** -0.5)
    k = head_cols(qk[..., D_A:2 * D_A])

    base = 4 * D_A
    f_raw = p_ref[cols(base + D_B, D_B)]
    logits = lbl_ref[...]
    ex = jnp.exp(logits - jnp.max(logits, axis=0, keepdims=True))
    lb = ex[0:1, :] / jnp.sum(ex, axis=0, keepdims=True)
    f = lb + (1.0 - lb) * _sigmoid(f_raw)
    lf = jnp.log(f)

    lf_all = _log_sigmoid(g)
    if batched or tt == SUBLANES:
        b = _sublane_cumsum(lf, row)
        Fc, uc, ur = _gate_vectors_small(cat_heads(lambda hd: g[..., hd:hd + 1]),
                                         cat_heads(lambda hd: lf_all[..., N_HEADS + hd:N_HEADS + hd + 1]),
                                         tril, eye)
    else:
        tril_bf16 = jnp.where(tril, 1.0, 0.0).astype(_BF16)
        triu_bf16 = jnp.where(ci >= ri, 1.0, 0.0).astype(_BF16)
        cum = _tril_matmul_cumsum(jnp.concatenate([lf, lf_all], axis=1), tril_bf16)
        b = cum[:, 0:D_B]
        g_t = g.T
        F_rows = _triu_matmul_cumsum(_log_sigmoid(g_t[0:2 * N_HEADS, :]), triu_bf16)
        Fc = cat_heads(lambda hd: cum[:, D_B + N_HEADS + hd:D_B + N_HEADS + hd + 1])
        uc = cat_heads(lambda hd: g[:, hd:hd + 1]) - Fc
        ur = cat_heads(lambda hd: g_t[hd:hd + 1, :] - F_rows[N_HEADS + hd:N_HEADS + hd + 1, :])

    fill()
    v = head_cols(p_ref[cols(2 * D_A, D_A)])
    o_raw = head_cols(p_ref[cols(3 * D_A, D_A)])
    C0 = cat_heads(lambda hd: C_in[head(hd)])
    n0 = cat_heads(lambda hd: n_in[head_row(hd)])
    m0 = cat_heads(m_in)
    h, C_new, n_new, m_new = _mlstm_head(q, k, v, Fc, uc, ur, C0, n0, m0, tril, row, t_last, fill)
    out_a = _sigmoid(o_raw) * _headnorm(h, head_gain(mnorm_ref))
    m_new = jnp.broadcast_to(m_new, m_new.shape[:-1] + (GATE_COLS,))
    for hd in range(N_HEADS):
        C_out[head(hd)] = of_head(C_new, hd)
        n_out[head_row(hd)] = of_head(n_new, hd)
        m_out[head_row(hd)] = of_head(m_new, hd)
        mix_ref[cols(hd * DH, DH)] = of_head(out_a, hd).astype(mix_ref.dtype)

    q_raw = p_ref[cols(base, D_B)]
    g_raw = p_ref[cols(base + 3 * D_B, D_B)]
    kk = (1.0 - lb) * _sigmoid(-f_raw)
    qs = q_raw * _sigmoid(q_raw)
    S0 = cat_heads(lambda hd: S_in[head(hd)])
    o, S_new = _hgrn_core(head_cols(qs), head_cols(kk), head_cols(p_ref[cols(base + 2 * D_B, D_B)]),
                          head_cols(f), head_cols(b), S0, eye_dh, xor_rc, row, t_last,
                          b_ref, qt_ref, kt_ref, nd_ref, fill)
    out_b = _headnorm(o, head_gain(hnorm_ref)) * head_cols(g_raw * _sigmoid(g_raw))
    for hd in range(N_HEADS):
        S_out[head(hd)] = of_head(S_new, hd)
        mix_ref[cols(D_A + hd * DH, DH)] = of_head(out_b, hd).astype(mix_ref.dtype)


def _project(x, nm_ref, wa_ref, wb_ref, bmain_ref, wg_ref, bg_ref):
    h = _rmsnorm(x, nm_ref[...]).astype(_BF16)
    half = P_MAIN // 2
    p = jnp.concatenate([_dot_nt(h, wa_ref[...]) + bmain_ref[:, 0:half],
                         _dot_nt(h, wb_ref[...]) + bmain_ref[:, half:P_MAIN]], axis=1)
    g = _dot_nt(h, wg_ref[...]) + bg_ref[...]
    return p, g


def _layer_kernel(x_ref, xprev_ref, nm_ref, wa_ref, wb_ref, bmain_ref, wg_ref, bg_ref, convw_ref, convb_ref,
                  mnorm_ref, lbl_ref, hnorm_ref, wout_ref, nf_ref, wgate_ref, wup_ref, wdown_ref, nfin_ref,
                  y_ref, convp_ref, C_ref, n_ref, m_ref, S_ref,
                  p_scr, g_scr, mix_scr, convbuf, b_ref, qt_ref, kt_ref, nd_ref, n_tiles, tiles_per_row):
    tt = x_ref.shape[0]
    t = pl.program_id(0)
    ff_edges = FFN_CHUNK_EDGES
    fill_at = FFN_PIECE_POINTS

    def ffn_pieces():
        held = {}

        def out_proj():
            x1 = xprev_ref[...] + _dot(mix_scr[...], wout_ref[...])
            held.update(x1=x1, h2=_rmsnorm(x1, nf_ref[...]).astype(_BF16))

        def gate(c):
            cs = pl.ds(ff_edges[c], ff_edges[c + 1] - ff_edges[c])

            def emit():
                held["gate"] = _dot(held["h2"], wgate_ref[:, cs])
            return emit

        def up_down(c):
            cs = pl.ds(ff_edges[c], ff_edges[c + 1] - ff_edges[c])

            def emit():
                gt = held["gate"]
                ff = (gt * _sigmoid(gt) * _dot(held["h2"], wup_ref[:, cs])).astype(_BF16)
                part = _dot(ff, wdown_ref[cs, :])
                held["x2"] = (held["x1"] if c == 0 else held["x2"]) + part
            return emit

        def final():
            y_ref[...] = _rmsnorm(held["x2"], nfin_ref[...])

        pieces = [out_proj]
        for c in range(len(ff_edges) - 1):
            pieces += [gate(c), up_down(c)]
        return pieces + [final]

    @pl.when(t == 0)
    def _():
        mix_scr[...] = jnp.zeros_like(mix_scr)

    @pl.when(jnp.logical_and(t < n_tiles, t % tiles_per_row == 0))
    def _():
        C_ref[...] = jnp.zeros_like(C_ref)
        n_ref[...] = jnp.zeros_like(n_ref)
        m_ref[...] = jnp.zeros_like(m_ref)
        S_ref[...] = jnp.zeros_like(S_ref)
        convbuf[0:CONV_PAD, :] = jnp.zeros((CONV_PAD, 2 * D_A), _F32)

    @pl.when(t < n_tiles)
    def _():
        pieces = ffn_pieces()
        calls = [0]

        def fill():
            for _ in range(fill_at.count(calls[0])):
                pieces.pop(0)()
            calls[0] += 1

        fill()
        p, g = _project(x_ref[...], nm_ref, wa_ref, wb_ref, bmain_ref, wg_ref, bg_ref)
        p_scr[...] = p
        g_scr[...] = g
        ck = CHUNK_PROMPT
        for r0 in range(0, tt, ck):
            _mixers(p_scr.at[pl.ds(r0, ck)], g_scr[pl.ds(r0, ck), :], convw_ref, convb_ref, mnorm_ref, lbl_ref,
                    hnorm_ref, mix_scr.at[pl.ds(r0, ck)], convp_ref, C_ref, n_ref,
                    lambda hd: m_ref[hd:hd + 1, 0:1], S_ref, C_ref, n_ref, m_ref, S_ref,
                    convbuf, b_ref, qt_ref, kt_ref, nd_ref, ck, ck - 1, fill)
        while pieces:
            pieces.pop(0)()

    @pl.when(t == n_tiles)
    def _():
        for piece in ffn_pieces():
            piece()


def _proj_kernel(x_ref, nm_ref, wa_ref, wb_ref, bmain_ref, wg_ref, bg_ref, p_ref, g_ref):
    p, g = _project(x_ref[...], nm_ref, wa_ref, wb_ref, bmain_ref, wg_ref, bg_ref)
    p_ref[...] = p
    g_ref[...] = g


def _sample_kernel(p_ref, g_ref, conv0_ref, C0_ref, n0_ref, m0_ref, S0_ref,
                   convw_ref, convb_ref, mnorm_ref, lbl_ref, hnorm_ref,
                   mix_ref, convs_ref, C_ref, n_ref, m_ref, S_ref,
                   convbuf, t_last):
    n_seq, tt, _ = p_ref.shape
    lead_rows = CONV_PAD - (CONV_W - 1)
    convbuf[:, 0:lead_rows, :] = jnp.zeros((n_seq, lead_rows, 2 * D_A), _F32)
    convbuf[:, lead_rows:CONV_PAD, :] = conv0_ref[...]
    m_ref[...] = jnp.zeros_like(m_ref)
    _mixers(p_ref, g_ref[...], convw_ref, convb_ref, mnorm_ref, lbl_ref, hnorm_ref,
            mix_ref, convs_ref, C0_ref, n0_ref, lambda hd: m0_ref[:, hd:hd + 1, :], S0_ref,
            C_ref, n_ref, m_ref, S_ref, convbuf, None, None, None, None, tt, t_last)


def _ffn_kernel(x_ref, mix_ref, wout_ref, nf_ref, wgate_ref, wup_ref, wdown_ref, nfin_ref, y_ref):
    x1 = x_ref[...] + _dot(mix_ref[...], wout_ref[...])
    h2 = _rmsnorm(x1, nf_ref[...]).astype(_BF16)
    gate = _dot(h2, wgate_ref[...])
    up = _dot(h2, wup_ref[...])
    ff = (gate * _sigmoid(gate) * up).astype(_BF16)
    x2 = x1 + _dot(ff, wdown_ref[...])
    y_ref[...] = _rmsnorm(x2, nfin_ref[...])


def _resident(shape):
    nd = len(shape)
    return pl.BlockSpec(shape, lambda *_: (0,) * nd, pipeline_mode=pl.Buffered(1))


def _prompt_layer(x, mixer_w, ffn_w):
    B, T, D = x.shape
    tt = TT_PROMPT
    ck = CHUNK_PROMPT
    n_tiles = B * T // tt
    tiles_per_row = T // tt
    x2 = x.reshape(B * T, D)
    weights = tuple(mixer_w) + tuple(ffn_w)

    def cur(t):
        return jnp.minimum(t, n_tiles - 1)

    def prev(t):
        return jnp.maximum(t - 1, 0)

    def row(t):
        return cur(t) // tiles_per_row

    out_shape = (
        jax.ShapeDtypeStruct((B * T, D), _F32),
        jax.ShapeDtypeStruct((B, CONV_W - 1, 2 * D_A), _F32),
        jax.ShapeDtypeStruct((B, N_HEADS, DH, DH), _F32),
        jax.ShapeDtypeStruct((B, N_HEADS, DH), _F32),
        jax.ShapeDtypeStruct((B, SUBLANES, GATE_COLS), _F32),
        jax.ShapeDtypeStruct((B, N_HEADS, DH, DH), _F32),
    )
    in_specs = [
        pl.BlockSpec((tt, D), lambda t: (cur(t), 0)),
        pl.BlockSpec((tt, D), lambda t: (prev(t), 0)),
    ] + [_resident(w.shape) for w in weights]
    out_specs = (
        pl.BlockSpec((tt, D), lambda t: (prev(t), 0)),
        pl.BlockSpec((None, CONV_W - 1, 2 * D_A), lambda t: (row(t), 0, 0)),
        pl.BlockSpec((None, N_HEADS, DH, DH), lambda t: (row(t), 0, 0, 0)),
        pl.BlockSpec((None, N_HEADS, DH), lambda t: (row(t), 0, 0)),
        pl.BlockSpec((None, SUBLANES, GATE_COLS), lambda t: (row(t), 0, 0)),
        pl.BlockSpec((None, N_HEADS, DH, DH), lambda t: (row(t), 0, 0, 0)),
    )
    scratch = [
        pltpu.VMEM((tt, P_MAIN), _F32), pltpu.VMEM((tt, GATE_COLS), _F32),
        pltpu.VMEM((tt, D_MODEL), _BF16),
        pltpu.VMEM((CONV_PAD + ck, 2 * D_A), _F32),
        pltpu.VMEM((N_HEADS, ck, DH), _F32),
        pltpu.VMEM((N_HEADS, ck, DH), _F32),
        pltpu.VMEM((N_HEADS, ck, DH), _F32),
        pltpu.VMEM((5, N_HEADS, ck, DH), _F32),
    ]
    return pl.pallas_call(
        functools.partial(_layer_kernel, n_tiles=n_tiles, tiles_per_row=tiles_per_row),
        out_shape=out_shape, grid=(n_tiles + 1,), in_specs=in_specs, out_specs=out_specs,
        scratch_shapes=scratch, name="prompt_layer",
        compiler_params=pltpu.CompilerParams(dimension_semantics=("arbitrary",), vmem_limit_bytes=VMEM_LIMIT),
    )(x2, x2, *weights)


def _sample_proj(x, nm, wa, wb, bmain, wg, bg):
    N, D = x.shape
    tm = min(N, 512)
    return pl.pallas_call(
        _proj_kernel,
        out_shape=(jax.ShapeDtypeStruct((N, P_MAIN), _F32), jax.ShapeDtypeStruct((N, GATE_COLS), _F32)),
        grid=(N // tm,),
        in_specs=[pl.BlockSpec((tm, D), lambda i: (i, 0))] + [_resident(a.shape) for a in (nm, wa, wb, bmain, wg, bg)],
        out_specs=(pl.BlockSpec((tm, P_MAIN), lambda i: (i, 0)), pl.BlockSpec((tm, GATE_COLS), lambda i: (i, 0))),
        name="sample_proj",
        compiler_params=pltpu.CompilerParams(dimension_semantics=("arbitrary",), vmem_limit_bytes=VMEM_LIMIT),
    )(x, nm, wa, wb, bmain, wg, bg)


def _sample_mixer(p, g, conv0, C0, n0, m0, S0, convw, convb, mnorm, lbl, hnorm, t_last):
    B, tt, _ = p.shape
    out_shape = (
        jax.ShapeDtypeStruct((B, tt, D_MODEL), _BF16),
        jax.ShapeDtypeStruct((B, CONV_W - 1, 2 * D_A), _F32),
        jax.ShapeDtypeStruct((B, N_HEADS, DH, DH), _F32),
        jax.ShapeDtypeStruct((B, N_HEADS, DH), _F32),
        jax.ShapeDtypeStruct((B, SUBLANES, GATE_COLS), _F32),
        jax.ShapeDtypeStruct((B, N_HEADS, DH, DH), _F32),
    )
    ns = SEQS_PER_STEP
    in_specs = [
        pl.BlockSpec((ns, tt, P_MAIN), lambda b: (b, 0, 0)),
        pl.BlockSpec((ns, tt, GATE_COLS), lambda b: (b, 0, 0)),
        pl.BlockSpec((ns, CONV_W - 1, 2 * D_A), lambda b: (b, 0, 0)),
        pl.BlockSpec((ns, N_HEADS, DH, DH), lambda b: (b, 0, 0, 0)),
        pl.BlockSpec((ns, N_HEADS, DH), lambda b: (b, 0, 0)),
        pl.BlockSpec((ns, N_HEADS, 1), lambda b: (b, 0, 0)),
        pl.BlockSpec((ns, N_HEADS, DH, DH), lambda b: (b, 0, 0, 0)),
        _resident(convw.shape), _resident(convb.shape), _resident(mnorm.shape), _resident(lbl.shape),
        _resident(hnorm.shape),
    ]
    out_specs = (
        pl.BlockSpec((ns, tt, D_MODEL), lambda b: (b, 0, 0)),
        pl.BlockSpec((ns, CONV_W - 1, 2 * D_A), lambda b: (b, 0, 0)),
        pl.BlockSpec((ns, N_HEADS, DH, DH), lambda b: (b, 0, 0, 0)),
        pl.BlockSpec((ns, N_HEADS, DH), lambda b: (b, 0, 0)),
        pl.BlockSpec((ns, SUBLANES, GATE_COLS), lambda b: (b, 0, 0)),
        pl.BlockSpec((ns, N_HEADS, DH, DH), lambda b: (b, 0, 0, 0)),
    )
    return pl.pallas_call(
        functools.partial(_sample_kernel, t_last=t_last),
        out_shape=out_shape, grid=(B // ns,), in_specs=in_specs, out_specs=out_specs,
        scratch_shapes=[pltpu.VMEM((ns, CONV_PAD + tt, 2 * D_A), _F32)], name="sample_mixer",
        compiler_params=pltpu.CompilerParams(dimension_semantics=("arbitrary",), vmem_limit_bytes=VMEM_LIMIT),
    )(p, g, conv0, C0, n0, m0, S0, convw, convb, mnorm, lbl, hnorm)


def _ffn(x, mix, wout, nf, wgate, wup, wdown, nfin):
    N, D = x.shape
    tm = min(N, TM_FFN)
    return pl.pallas_call(
        _ffn_kernel, out_shape=jax.ShapeDtypeStruct((N, D), _F32), grid=(N // tm,),
        in_specs=[pl.BlockSpec((tm, D), lambda i: (i, 0)), pl.BlockSpec((tm, D_MODEL), lambda i: (i, 0)),
                  _resident(wout.shape), _resident(nf.shape), _resident(wgate.shape), _resident(wup.shape),
                  _resident(wdown.shape), _resident(nfin.shape)],
        out_specs=pl.BlockSpec((tm, D), lambda i: (i, 0)), name="ffn",
        compiler_params=pltpu.CompilerParams(dimension_semantics=("arbitrary",), vmem_limit_bytes=VMEM_LIMIT),
    )(x, mix, wout, nf, wgate, wup, wdown, nfin)


def kernel(x_prompt, x_sample, state_conv, state_mlstm_C, state_mlstm_n, state_mlstm_m, state_hgrn_S,
           norm_mix, w_in, b_in, conv_w, conv_b, mlstm_norm, hgrn_lb_logits, hgrn_norm, w_out,
           norm_ffn, w_gate, w_up, w_down, norm_final):
    depth = w_in.shape[0]
    assert depth == 1, "single-layer stack"
    B, T, D = x_prompt.shape
    BS, TS, _ = x_sample.shape
    l = 0
    g0 = 4 * D_A
    n_gates = 2 * N_HEADS

    wt = jnp.swapaxes(w_in[l], 0, 1)
    wa = wt[:g0].astype(_BF16)
    wb = wt[g0 + n_gates:].astype(_BF16)
    wg = jnp.pad(wt[g0:g0 + n_gates], ((0, GATE_COLS - n_gates), (0, 0))).astype(_BF16)
    bmain = jnp.concatenate([b_in[l, :g0], b_in[l, g0 + n_gates:]])[None, :]
    bg = jnp.pad(b_in[l, g0:g0 + n_gates], (0, GATE_COLS - n_gates))[None, :]
    nm = norm_mix[l][None, :]
    convw = conv_w[l]
    convb = conv_b[l][None, :]
    mnorm = mlstm_norm[l][None, :]
    hnorm = hgrn_norm[l][None, :]
    lbl = hgrn_lb_logits.astype(_F32)
    wout = w_out[l].astype(_BF16)
    wgate = w_gate[l].astype(_BF16)
    wup = w_up[l].astype(_BF16)
    wdown = w_down[l].astype(_BF16)
    nf = norm_ffn[l][None, :]
    nfin = norm_final[None, :]

    mixer_w = (nm, wa, wb, bmain, wg, bg, convw, convb, mnorm, lbl, hnorm)
    ffn_w = (wout, nf, wgate, wup, wdown, nfin)
    y_p, conv_p, C_p, n_p, m_p, S_p = _prompt_layer(x_prompt, mixer_w, ffn_w)
    y_prompt = y_p.reshape(B, T, D)

    xs_pad = jnp.pad(x_sample, ((0, 0), (0, TT_SAMPLE - TS), (0, 0)))
    p_s, g_s = _sample_proj(xs_pad.reshape(BS * TT_SAMPLE, D), nm, wa, wb, bmain, wg, bg)
    mix_s, conv_s, C_s, n_s, m_s, S_s = _sample_mixer(
        p_s.reshape(BS, TT_SAMPLE, P_MAIN), g_s.reshape(BS, TT_SAMPLE, GATE_COLS), state_conv[l],
        state_mlstm_C[l], state_mlstm_n[l], state_mlstm_m[l][:, :, None], state_hgrn_S[l],
        convw, convb, mnorm, lbl, hnorm, TS - 1)
    y_sample = _ffn(x_sample.reshape(BS * TS, D), mix_s[:, :TS, :].reshape(BS * TS, D_MODEL), wout, nf, wgate,
                    wup, wdown, nfin).reshape(BS, TS, D)

    return (y_prompt, y_sample,
            conv_p[None], C_p[None], n_p[None], m_p[None, :, :N_HEADS, 0], S_p[None],
            conv_s[None], C_s[None], n_s[None], m_s[None, :, :N_HEADS, 0], S_s[None])
```
